```python
import jax, jax.numpy as jnp
from jax import lax
import numpy as np

D_MODEL = 1024
BATCH = 4
SEQ = 4096
DEPTH = 4
DEC_BATCH = 128
DEC_SEQ = 8
PAST_LEN = 2048
PAGE_SIZE = 128

N_MIXERS = 4
N_RWKV = len(range(0, DEPTH, N_MIXERS))
N_GLA = len(range(1, DEPTH, N_MIXERS))
N_SB = len(range(2, DEPTH, N_MIXERS))
N_HG = len(range(3, DEPTH, N_MIXERS))
RW_HEAD = 64
RW_HEADS = D_MODEL // RW_HEAD
RW_DECAY_LORA = 64
RW_A_LORA = 64
RW_GN_EPS = 64e-5
GLA_HEADS = 4
GLA_DK = D_MODEL // 2 // GLA_HEADS
GLA_DV = D_MODEL // GLA_HEADS
GLA_GATE_LORA = 16
GLA_GATE_NORMALIZER = 16.0
SB_HEADS = 16
SB_HEAD = D_MODEL // SB_HEADS
SB_BLOCK = 128
SB_BIAS_INIT = -7.0
HG_EXPAND = 128
HG_HEADS = D_MODEL // HG_EXPAND
HG_DV = D_MODEL // HG_HEADS
CHUNK = 64
NORM_EPS = 1e-6

kernel_name = 'hybrid_rwkv7_gla_stickbreak_hgrn2_step'

F32 = jnp.float32


def rmsnorm(x, g):
    xf = x.astype(F32)
    y = xf * lax.rsqrt(jnp.mean(xf * xf, axis=-1, keepdims=True) + NORM_EPS)
    return (y * g.astype(F32)).astype(x.dtype)


def ada_modulation(c, w, b):
    mod = jax.nn.silu(c) @ w + b
    shift, scale, gate = jnp.split(mod, 3, axis=-1)
    return shift[:, None], scale[:, None], gate[:, None]


def group_norm_heads(o, w, b):
    B, T, H, N = o.shape
    of = o.astype(F32)
    mu = jnp.mean(of, axis=-1, keepdims=True)
    var = jnp.mean(jnp.square(of - mu), axis=-1, keepdims=True)
    y = ((of - mu) * lax.rsqrt(var + RW_GN_EPS)).reshape(B, T, H * N)
    return y * w.astype(F32) + b.astype(F32)


def rwkv7_recurrence(r, decay, k, v, kk, a, S0):
    def step(S, inp):
        r_t, w_t, k_t, v_t, kk_t, a_t = inp
        sa = jnp.einsum('bhvk,bhk->bhv', S, -kk_t)
        S = (S * w_t[:, :, None, :] + sa[..., None] * (kk_t * a_t)[:, :, None, :]
             + v_t[..., None] * k_t[:, :, None, :])
        return S, jnp.einsum('bhvk,bhk->bhv', S, r_t)
    xs = tuple(jnp.moveaxis(t.astype(F32), 1, 0) for t in (r, decay, k, v, kk, a))
    S, o = lax.scan(step, S0.astype(F32), xs)
    return jnp.moveaxis(o, 0, 1), S


def rwkv7_mixer(h, x_prev, S0, mix, w_rkvg, w0, w1, w2, a0, a1, a2, k_k, k_a, r_k, gn_w, gn_b, w_o):
    B, T, D = h.shape
    xx = jnp.concatenate([x_prev[:, None].astype(h.dtype), h[:, :-1]], axis=1) - h
    xr, xw, xk, xv, xa, xg = (h + xx * mix[n] for n in range(6))
    r = xr @ w_rkvg[0]
    k = xk @ w_rkvg[1]
    v = xv @ w_rkvg[2]
    gate = jax.nn.silu(xg @ w_rkvg[3])
    log_w = -jax.nn.softplus(-(w0 + jnp.tanh(xw @ w1) @ w2).astype(F32)) - 0.5
    decay = jnp.exp(-jnp.exp(log_w))
    a = jax.nn.sigmoid(a0 + (xa @ a1) @ a2)
    heads = lambda t: t.reshape(B, T, RW_HEADS, RW_HEAD)
    kk = heads(k * k_k).astype(F32)
    kk = kk / jnp.maximum(jnp.sqrt(jnp.sum(kk * kk, axis=-1, keepdims=True)), 1e-12)
    k = k * (1 + (a - 1) * k_a)
    r_h, k_h, v_h = heads(r), heads(k), heads(v)
    o, S = rwkv7_recurrence(r_h, heads(decay), k_h, v_h, kk, heads(a), S0)
    o = group_norm_heads(o, gn_w, gn_b)
    bonus = jnp.sum((r_h * k_h * r_k.reshape(RW_HEADS, RW_HEAD)).astype(F32), axis=-1, keepdims=True) * v_h.astype(F32)
    o = (o + bonus.reshape(B, T, D)) * gate.astype(F32)
    return o.astype(h.dtype) @ w_o, h[:, -1], S


def chunk_gated_linear_attn(q, k, v, g, S0):
    B, T, H, K = q.shape
    V = v.shape[-1]
    C = CHUNK if T % CHUNK == 0 else T
    N = T // C
    to_chunks = lambda t: jnp.moveaxis(t.astype(F32).reshape(B, N, C, H, t.shape[-1]), 1, 0)
    causal = jnp.tril(jnp.ones((C, C), dtype=bool))

    def step(S, inp):
        q_c, k_c, v_c, g_c = inp
        b = jnp.cumsum(g_c, axis=1)
        rel = jnp.where(causal[None, :, :, None, None], b[:, :, None] - b[:, None, :], -jnp.inf)
        attn = jnp.sum(q_c[:, :, None] * k_c[:, None] * jnp.exp(rel), axis=-1)
        o = (jnp.einsum('btsh,bshv->bthv', attn, v_c)
             + jnp.einsum('bthk,bhkv->bthv', q_c * jnp.exp(b), S))
        b_last = b[:, -1]
        S = (S * jnp.exp(b_last)[..., None]
             + jnp.einsum('bshk,bshv->bhkv', k_c * jnp.exp(b_last[:, None] - b), v_c))
        return S, o

    S, o = lax.scan(step, S0.astype(F32), tuple(map(to_chunks, (q, k, v, g))))
    return jnp.moveaxis(o, 0, 1).reshape(B, T, H, V), S


def gla_mixer(h, S0, w_in, w_gk2, b_gk2, gn_w, w_o):
    B, T, D = h.shape
    qk = GLA_HEADS * GLA_DK
    q, k, v, gate, gk_low = jnp.split(h @ w_in, [qk, 2 * qk, 2 * qk + D, 2 * qk + 2 * D], axis=-1)
    gk = jax.nn.log_sigmoid((gk_low @ w_gk2 + b_gk2).astype(F32)) / GLA_GATE_NORMALIZER
    kh = lambda t: t.reshape(B, T, GLA_HEADS, GLA_DK)
    o, S = chunk_gated_linear_attn(kh(q) * GLA_DK ** -0.5, kh(k), v.reshape(B, T, GLA_HEADS, GLA_DV), kh(gk), S0)
    o = rmsnorm(o, gn_w).reshape(B, T, D) * jax.nn.silu(gate.astype(F32))
    return o.astype(h.dtype) @ w_o, S


def hgrn2_mixer(h, S0, lb, w_in, gn_w, w_o):
    B, T, D = h.shape
    q, f, i, gate = jnp.split(h @ w_in, 4, axis=-1)
    lb = lb.astype(F32)
    forget = lb + (1 - lb) * jax.nn.sigmoid(f.astype(F32))
    fh = lambda t: t.reshape(B, T, HG_HEADS, HG_EXPAND)
    o, S = chunk_gated_linear_attn(fh(jax.nn.silu(q)) * HG_EXPAND ** -0.5, fh(1 - forget),
                                   i.reshape(B, T, HG_HEADS, HG_DV), fh(jnp.log(forget)), S0)
    o = rmsnorm(o, gn_w).reshape(B, T, D) * jax.nn.silu(gate.astype(F32))
    return o.astype(h.dtype) @ w_o, S


def stick_breaking_block(q, k, v, bias, q_pos, k_pos):
    z = jnp.einsum('bthd,bshd->bhts', q, k, preferred_element_type=F32) + bias.astype(F32)[None, :, None, None]
    before = k_pos[None, :] < q_pos[:, None]
    log_1mb = jnp.where(before, jax.nn.log_sigmoid(-z), 0.0)
    tail = lax.cumsum(log_1mb, axis=3, reverse=True) - log_1mb
    weight = jnp.exp(jnp.where(before, jax.nn.log_sigmoid(z) + tail, -jnp.inf))
    return jnp.einsum('bhts,bshd->bthd', weight, v.astype(F32))


def stick_breaking_attention(q, k, v, k_past, v_past, bias):
    P, T = k_past.shape[1], q.shape[1]
    k_all = jnp.concatenate([k_past, k], axis=1)
    v_all = jnp.concatenate([v_past, v], axis=1)
    outs = []
    for start in range(0, T, SB_BLOCK):
        stop = min(start + SB_BLOCK, T)
        q_pos = P + jnp.arange(start, stop)
        k_pos = jnp.arange(P + stop)
        outs.append(stick_breaking_block(q[:, start:stop], k_all[:, :P + stop], v_all[:, :P + stop], bias, q_pos, k_pos))
    return jnp.concatenate(outs, axis=1)


def stick_breaking_mixer(h, k_past, v_past, w_in, bias, w_o):
    B, T, D = h.shape
    q, k, v, gate = jnp.split(h @ w_in, 4, axis=-1)
    heads = lambda t: t.reshape(B, T, SB_HEADS, SB_HEAD)
    q, k, v = heads(q) * SB_HEAD ** -0.5, heads(k), heads(v)
    o = stick_breaking_attention(q, k, v, k_past.astype(k.dtype), v_past.astype(v.dtype), bias)
    o = o.reshape(B, T, D) * jax.nn.silu(gate.astype(F32))
    return o.astype(h.dtype) @ w_o, k, v


def gather_pages(pool, page_table):
    rows = pool[page_table]
    return rows.reshape(page_table.shape[0], -1, pool.shape[2], pool.shape[3])


def setup_inputs(seed: int = 0) -> dict:
    key = jax.random.key(seed)
    ks = iter(jax.random.split(key, 64))
    nrm = lambda shape, s=1.0: jax.random.normal(next(ks), shape, F32) * s
    unif = lambda shape: jax.random.uniform(next(ks), shape, F32)
    D = D_MODEL
    n_pages = PAST_LEN // PAGE_SIZE
    n_used = DEC_BATCH * n_pages
    n_pool = n_used + (n_used + 3) // 4
    perm = jax.random.permutation(next(ks), n_pool)
    page_table = perm[:n_used].reshape(DEC_BATCH, n_pages).astype(jnp.int32)
    qk = GLA_HEADS * GLA_DK
    return {
        'x_prompt': nrm((BATCH, SEQ, D)),
        'x_sample': nrm((DEC_BATCH, DEC_SEQ, D)),
        'c_prompt': nrm((BATCH, D)),
        'c_sample': nrm((DEC_BATCH, D)),
        'state_rwkv': nrm((N_RWKV, DEC_BATCH, RW_HEADS, RW_HEAD, RW_HEAD), 0.3),
        'cache_rwkv_shift': nrm((N_RWKV, DEC_BATCH, D)),
        'state_gla': nrm((N_GLA, DEC_BATCH, GLA_HEADS, GLA_DK, GLA_DV), 0.5),
        'cache_sb_k': nrm((N_SB, n_pool, PAGE_SIZE, SB_HEADS, SB_HEAD)),
        'cache_sb_v': nrm((N_SB, n_pool, PAGE_SIZE, SB_HEADS, SB_HEAD)),
        'state_hgrn': nrm((N_HG, DEC_BATCH, HG_HEADS, HG_EXPAND, HG_DV), 0.5),
        'page_table': page_table,
        'norm_g': 1.0 + nrm((DEPTH, D), 0.02),
        'ada_w': nrm((DEPTH, D, 3 * D), 0.5 * D ** -0.5),
        'ada_b': nrm((DEPTH, 3 * D), 0.02),
        'final_g': 1.0 + nrm((D,), 0.02),
        'rw_mix': unif((N_RWKV, 6, D)),
        'rw_w_rkvg': nrm((N_RWKV, 4, D, D), D ** -0.5),
        'rw_w0': nrm((N_RWKV, D), 0.5),
        'rw_w1': nrm((N_RWKV, D, RW_DECAY_LORA), D ** -0.5),
        'rw_w2': nrm((N_RWKV, RW_DECAY_LORA, D), 0.3 * RW_DECAY_LORA ** -0.5),
        'rw_a0': nrm((N_RWKV, D), 0.1),
        'rw_a1': nrm((N_RWKV, D, RW_A_LORA), D ** -0.5),
        'rw_a2': nrm((N_RWKV, RW_A_LORA, D), 0.3 * RW_A_LORA ** -0.5),
        'rw_k_k': 0.85 + nrm((N_RWKV, D), 0.02),
        'rw_k_a': 1.0 + nrm((N_RWKV, D), 0.02),
        'rw_r_k': nrm((N_RWKV, D), 0.1),
        'rw_gn_w': 1.0 + nrm((N_RWKV, D), 0.02),
        'rw_gn_b': nrm((N_RWKV, D), 0.02),
        'rw_w_o': nrm((N_RWKV, D, D), D ** -0.5),
        'gla_w_in': nrm((N_GLA, D, 2 * qk + 2 * D + GLA_GATE_LORA), D ** -0.5),
        'gla_w_gk2': nrm((N_GLA, GLA_GATE_LORA, qk), GLA_GATE_LORA ** -0.5),
        'gla_b_gk2': nrm((N_GLA, qk), 0.1),
        'gla_gn_w': 1.0 + nrm((N_GLA, GLA_DV), 0.02),
        'gla_w_o': nrm((N_GLA, D, D), D ** -0.5),
        'sb_w_in': nrm((N_SB, D, 4 * D), D ** -0.5),
        'sb_bias': SB_BIAS_INIT + nrm((N_SB, SB_HEADS), 0.5),
        'sb_w_o': nrm((N_SB, D, D), D ** -0.5),
        'hg_w_in': nrm((N_HG, D, 4 * D), D ** -0.5),
        'hg_lower': nrm((DEPTH, D), 0.1),
        'hg_gn_w': 1.0 + nrm((N_HG, HG_DV), 0.02),
        'hg_w_o': nrm((N_HG, D, D), D ** -0.5),
    }


def reference(x_prompt, x_sample, c_prompt, c_sample, state_rwkv, cache_rwkv_shift, state_gla,
              cache_sb_k, cache_sb_v, state_hgrn, page_table, norm_g, ada_w, ada_b, final_g,
              rw_mix, rw_w_rkvg, rw_w0, rw_w1, rw_w2, rw_a0, rw_a1, rw_a2, rw_k_k, rw_k_a, rw_r_k,
              rw_gn_w, rw_gn_b, rw_w_o, gla_w_in, gla_w_gk2, gla_b_gk2, gla_gn_w, gla_w_o,
              sb_w_in, sb_bias, sb_w_o, hg_w_in, hg_lower, hg_gn_w, hg_w_o):
    lb_soft = jax.nn.softmax(hg_lower.astype(F32), axis=0)
    lb_all = jnp.cumsum(lb_soft, axis=0) - lb_soft[0]

    def trunk(x, c, rw_S, rw_shift, gla_S, sb_past, hg_S):
        new_rw_S, new_rw_shift, new_gla_S, new_sb_k, new_sb_v, new_hg_S = [], [], [], [], [], []
        for i in range(DEPTH):
            kind, j = i % N_MIXERS, i // N_MIXERS
            shift, scale, gate = ada_modulation(c, ada_w[i], ada_b[i])
            h = rmsnorm(x, norm_g[i]) * (1 + scale) + shift
            if kind == 0:
                out, sh, S = rwkv7_mixer(h, rw_shift[j], rw_S[j], rw_mix[j], rw_w_rkvg[j], rw_w0[j], rw_w1[j],
                                         rw_w2[j], rw_a0[j], rw_a1[j], rw_a2[j], rw_k_k[j], rw_k_a[j],
                                         rw_r_k[j], rw_gn_w[j], rw_gn_b[j], rw_w_o[j])
                new_rw_S.append(S)
                new_rw_shift.append(sh)
            elif kind == 1:
                out, S = gla_mixer(h, gla_S[j], gla_w_in[j], gla_w_gk2[j], gla_b_gk2[j], gla_gn_w[j], gla_w_o[j])
                new_gla_S.append(S)
            elif kind == 2:
                k_past, v_past = sb_past(j)
                out, k_new, v_new = stick_breaking_mixer(h, k_past, v_past, sb_w_in[j], sb_bias[j], sb_w_o[j])
                new_sb_k.append(k_new)
                new_sb_v.append(v_new)
            else:
                out, S = hgrn2_mixer(h, hg_S[j], lb_all[i], hg_w_in[j], hg_gn_w[j], hg_w_o[j])
                new_hg_S.append(S)
            x = x + gate * out
        dt = x.dtype
        stack = lambda xs: jnp.stack(xs).astype(dt)
        return (rmsnorm(x, final_g), stack(new_rw_S), stack(new_rw_shift), stack(new_gla_S),
                stack(new_sb_k), stack(new_sb_v), stack(new_hg_S))

    B = x_prompt.shape[0]
    dt = x_prompt.dtype
    empty_kv = jnp.zeros((B, 0, SB_HEADS, SB_HEAD), dt)
    (y_prompt, rwkv_state_p, rwkv_shift_p, gla_state_p, sb_k_p, sb_v_p, hgrn_state_p) = trunk(
        x_prompt, c_prompt,
        jnp.zeros((N_RWKV, B, RW_HEADS, RW_HEAD, RW_HEAD), dt),
        jnp.zeros((N_RWKV, B, D_MODEL), dt),
        jnp.zeros((N_GLA, B, GLA_HEADS, GLA_DK, GLA_DV), dt),
        lambda j: (empty_kv, empty_kv),
        jnp.zeros((N_HG, B, HG_HEADS, HG_EXPAND, HG_DV), dt))

    (y_sample, rwkv_state_s, rwkv_shift_s, gla_state_s, sb_k_s, sb_v_s, hgrn_state_s) = trunk(
        x_sample, c_sample, state_rwkv, cache_rwkv_shift, state_gla,
        lambda j: (gather_pages(cache_sb_k[j], page_table), gather_pages(cache_sb_v[j], page_table)),
        state_hgrn)

    return (y_prompt, y_sample, rwkv_state_p, rwkv_shift_p, gla_state_p, sb_k_p, sb_v_p, hgrn_state_p,
            rwkv_state_s, rwkv_shift_s, gla_state_s, sb_k_s, sb_v_s, hgrn_state_s)
```

```python
import functools
import math

import jax
import jax.numpy as jnp
from jax import lax
from jax.experimental import pallas as pl
from jax.experimental.pallas import tpu as pltpu

F32 = jnp.float32
BF16 = jnp.bfloat16

NORM_EPS = 1e-6
RW_HEAD = 64
RW_GN_EPS = 64e-5
GLA_HEADS = 4
GLA_GATE_NORMALIZER = 16.0
SB_HEADS = 16
HG_HEADS = 8
LANES = 128
ROW_TILE = 512
SUB = 16
VMEM_LIMIT = 56 * 1024 * 1024

_NN = (((1,), (0,)), ((), ()))
_NT = (((1,), (1,)), ((), ()))
_TN = (((0,), (0,)), ((), ()))


def _dot(a, b, dims=_NN):
    return lax.dot_general(a, b, dims, preferred_element_type=F32)


def _split(x):
    hi = x.astype(BF16)
    lo = (x - hi.astype(F32)).astype(BF16)
    return hi, lo


def _mm1(a, b, dims=_NN):
    return _dot(a.astype(BF16), b.astype(BF16), dims)


def _mm3(a, b, dims=_NN):
    ah, al = _split(a)
    bh, bl = _split(b)
    return _dot(ah, bh, dims) + (_dot(ah, bl, dims) + _dot(al, bh, dims))


def _mm_sel(a, sel, dims=_NN):
    ah, al = _split(a)
    return _dot(ah, sel, dims) + _dot(al, sel, dims)


def _sel_mm(sel, b, dims=_NN):
    bh, bl = _split(b)
    return _dot(sel, bh, dims) + _dot(sel, bl, dims)


def _sel_mm3(sel, b):
    b1 = b.astype(BF16)
    r1 = b - b1.astype(F32)
    b2 = r1.astype(BF16)
    b3 = (r1 - b2.astype(F32)).astype(BF16)
    return _dot(sel, b1) + (_dot(sel, b2) + _dot(sel, b3))


def _silu(x):
    return x * (1.0 / (1.0 + jnp.exp(-x)))


def _sigmoid(x):
    return 1.0 / (1.0 + jnp.exp(-x))


def _softplus(x):
    return jnp.maximum(x, 0.0) + jnp.log1p(jnp.exp(-jnp.abs(x)))


def _iota(shape, dim):
    return lax.broadcasted_iota(jnp.int32, shape, dim)


def _head_indicator(d, hs):
    return jnp.where(_iota((d, LANES), 0) // hs == _iota((d, LANES), 1), 1.0, 0.0).astype(BF16)


def _head_indicator_t(d, hs):
    return jnp.where(_iota((LANES, d), 1) // hs == _iota((LANES, d), 0), 1.0, 0.0).astype(BF16)


def _head_sum(x, hs):
    return _mm_sel(x, _head_indicator(x.shape[-1], hs))


def _head_bcast(s, d, hs):
    return _mm_sel(s, _head_indicator_t(d, hs))


def _row_tiles(B, T):
    tt = min(T, ROW_TILE)
    bb = max(1, min(B, ROW_TILE // tt))
    assert T % tt == 0 and B % bb == 0 and tt % 8 == 0
    return bb, tt


def _params(sem):
    return pltpu.CompilerParams(dimension_semantics=sem, vmem_limit_bytes=VMEM_LIMIT)


def _const_spec(shape):
    n = len(shape)
    return pl.BlockSpec(shape, lambda *_: (0,) * n)


def _prenorm(x, g, mod, d):
    shift = mod[:, :, 0:d]
    scale = mod[:, :, d:2 * d]
    y = x * lax.rsqrt(jnp.mean(x * x, axis=-1, keepdims=True) + NORM_EPS)
    return (y * g) * (1.0 + scale) + shift


def _mod_kernel(c_ref, w_ref, b_ref, o_ref):
    o_ref[0] = _mm1(_silu(c_ref[...]), w_ref[0]) + b_ref[0]


def _modulation(c, ada_w, ada_b):
    depth, d, n = ada_w.shape
    rows = c.shape[0]
    tn = 1536
    return pl.pallas_call(
        _mod_kernel,
        grid=(depth, n // tn),
        in_specs=[pl.BlockSpec((rows, d), lambda l, j: (0, 0)),
                  pl.BlockSpec((1, d, tn), lambda l, j: (l, 0, j)),
                  pl.BlockSpec((1, 1, tn), lambda l, j: (l, 0, j))],
        out_specs=pl.BlockSpec((1, rows, tn), lambda l, j: (l, 0, j)),
        out_shape=jax.ShapeDtypeStruct((depth, rows, n), F32),
        compiler_params=_params(("parallel", "parallel")),
        name="ada_modulation",
    )(c, ada_w, ada_b.reshape(depth, 1, n))


def _rwkv_pre_kernel(x_ref, halo_ref, xprev_ref, mod_ref, ng_ref, mix_ref, wr_ref, wk_ref, wv_ref, wg_ref,
                     w0_ref, w1_ref, w2_ref, a0_ref, a1_ref, a2_ref, kk_ref, ka_ref, rk_ref,
                     r_out, ld_out, k_out, v_out, kkn_out, a_out, g_out, bonus_out, shift_out):
    bb, tt, d = x_ref.shape
    m = bb * tt
    mod = mod_ref[...]
    g = ng_ref[...]
    h3 = _prenorm(x_ref[...], g, mod, d)
    h_halo = _prenorm(halo_ref[:, 7:8, :], g, mod, d)
    h_prev = jnp.where(pl.program_id(1) == 0, xprev_ref[...], h_halo)
    shift_out[...] = h3[:, tt - 1:tt, :]
    h = h3.reshape(m, d)
    prev = jnp.where(_iota((m, 1), 0) % tt == 0,
                     jnp.broadcast_to(h_prev, (bb, tt, d)).reshape(m, d),
                     pltpu.roll(h, 1, axis=0))
    xx = prev - h
    xr, xw, xk, xv, xa, xg = (h + xx * mix_ref[n:n + 1, :] for n in range(6))
    r = _mm1(xr, wr_ref[...])
    k = _mm1(xk, wk_ref[...])
    v = _mm1(xv, wv_ref[...])
    gate = _silu(_mm1(xg, wg_ref[...]))
    lw = w0_ref[...] + _mm1(jnp.tanh(_mm1(xw, w1_ref[...])), w2_ref[...])
    log_w = -_softplus(-lw) - 0.5
    a = _sigmoid(a0_ref[...] + _mm1(_mm1(xa, a1_ref[...]), a2_ref[...]))
    kk = k * kk_ref[...]
    nrm = jnp.maximum(jnp.sqrt(_head_sum(kk * kk, RW_HEAD)), 1e-12)
    kk = kk * _head_bcast(1.0 / nrm, d, RW_HEAD)
    k = k * (1.0 + (a - 1.0) * ka_ref[...])
    bonus = _head_bcast(_head_sum(r * k * rk_ref[...], RW_HEAD), d, RW_HEAD) * v
    for ref, val in ((r_out, r), (ld_out, -jnp.exp(log_w)), (k_out, k), (v_out, v), (kkn_out, kk),
                     (a_out, a), (g_out, gate), (bonus_out, bonus)):
        ref[...] = val.reshape(bb, tt, d)


def _pad_cols(w, n):
    return jnp.pad(w, ((0, 0), (0, n - w.shape[1])))


def _pad_rows(w, n):
    return jnp.pad(w, ((0, n - w.shape[0]), (0, 0)))


def _rwkv_pre(x, x_prev, mod, norm_g, p):
    B, T, d = x.shape
    tt = min(T, ROW_TILE // 2)
    bb = max(1, min(B, (ROW_TILE // 2) // tt))
    row = lambda a: a.reshape(1, d)
    tok = pl.BlockSpec((bb, tt, d), lambda i, j: (i, j, 0))
    per_seq = lambda n: pl.BlockSpec((bb, 1, n), lambda i, j: (i, 0, 0))
    halo = pl.BlockSpec((bb, 8, d), lambda i, j: (i, jnp.maximum(j * (tt // 8) - 1, 0), 0))
    consts = [row(norm_g), p['mix'],
              p['w_rkvg'][0].astype(BF16), p['w_rkvg'][1].astype(BF16), p['w_rkvg'][2].astype(BF16),
              p['w_rkvg'][3].astype(BF16),
              row(p['w0']), _pad_cols(p['w1'], LANES).astype(BF16), _pad_rows(p['w2'], LANES).astype(BF16),
              row(p['a0']), _pad_cols(p['a1'], LANES).astype(BF16), _pad_rows(p['a2'], LANES).astype(BF16),
              row(p['k_k']), row(p['k_a']), row(p['r_k'])]
    out = pl.pallas_call(
        _rwkv_pre_kernel,
        grid=(B // bb, T // tt),
        in_specs=[tok, halo, per_seq(d), per_seq(3 * d)] + [_const_spec(c.shape) for c in consts],
        out_specs=[tok] * 8 + [per_seq(d)],
        out_shape=[jax.ShapeDtypeStruct((B, T, d), F32)] * 8 + [jax.ShapeDtypeStruct((B, 1, d), F32)],
        compiler_params=_params(("parallel", "arbitrary")),
        name="rwkv_pre",
    )(x, x, x_prev.reshape(B, 1, d), mod, *consts)
    return out


def _unit_lower_inverse(m_strict, c, n):
    ti = _iota((n, n), 0)
    si = _iota((n, n), 1)
    eye = jnp.where(ti == si, 1.0, 0.0)
    sub = min(SUB, c)
    d_part = jnp.where(ti // sub == si // sub, m_strict, 0.0)
    t_d = eye + d_part
    pw = d_part
    for _ in range(int(math.log2(sub)) - 1):
        pw = _mm3(pw, pw)
        t_d = t_d + _mm3(t_d, pw)
    if sub == c:
        return t_d
    assert c // sub == 4
    nn = _mm3(t_d, m_strict - d_part)
    n2 = _mm3(nn, nn)
    return _mm3(eye + nn + n2 + _mm3(nn, n2), t_d)


def _rwkv_chunk_kernel(r_ref, ld_ref, k_ref, v_ref, kk_ref, a_ref, s0_ref, o_ref, s_out_ref, s_scr, *, c):
    ns, tb, d = r_ref.shape
    n = ns * c
    pairs = d // LANES
    half = LANES // 2
    j = pl.program_id(1)

    @pl.when(j == 0)
    def _():
        z = jnp.zeros((half, half), F32)
        for s in range(ns):
            for p in range(pairs):
                top = jnp.concatenate([s0_ref[s, 2 * p], z], axis=1)
                bot = jnp.concatenate([z, s0_ref[s, 2 * p + 1]], axis=1)
                s_scr[s, p] = jnp.concatenate([top, bot], axis=0)

    lane_a = _iota((n, LANES), 1) < half
    ti = _iota((2 * n, 2 * n), 0)
    si = _iota((2 * n, 2 * n), 1)
    same = ti // c == si // c
    strict = same & (si < ti)
    incl = same & (si <= ti)
    tc = _iota((n, n), 0)
    sc = _iota((n, n), 1)
    tri = jnp.where((tc // c == sc // c) & (sc <= tc), 1.0, 0.0).astype(BF16)
    last = jnp.where((tc // c == sc // c) & (sc % c == c - 1), 1.0, 0.0).astype(BF16)
    bd = (_iota((LANES, LANES), 0) < half) == (_iota((LANES, LANES), 1) < half)

    def load(ref, p):
        x = ref[:, :, p * LANES:(p + 1) * LANES]
        if tb < c:
            x = jnp.concatenate([x, jnp.zeros((ns, c - tb, LANES), F32)], axis=1)
        return x.reshape(n, LANES)

    def stack(x):
        return jnp.concatenate([jnp.where(lane_a, x, 0.0), jnp.where(lane_a, 0.0, x)], axis=0)

    def unstack(x):
        return x[0:n] + x[n:2 * n]

    for p in range(pairs):
        r, ld, k, v, kk, a = (load(ref, p) for ref in (r_ref, ld_ref, k_ref, v_ref, kk_ref, a_ref))
        cum = _sel_mm3(tri, ld)
        cum_last = _sel_mm3(last, cum)
        g_end = jnp.exp(cum_last - cum)
        g_inv = jnp.exp(-cum)
        beta = kk * a
        a_t = -kk * jnp.exp(cum - ld)
        r_t = r * jnp.exp(cum)
        k_t = k * g_inv
        b_t = beta * g_inv
        at_st = stack(a_t)
        rt_st = stack(r_t)
        k2 = jnp.concatenate([k_t, k_t], axis=0)
        b2 = jnp.concatenate([b_t, b_t], axis=0)
        m_k = jnp.where(strict, _mm3(at_st, k2, _NT), 0.0)
        m_b = jnp.where(strict, _mm3(at_st, b2, _NT), 0.0)
        a_k = jnp.where(incl, _mm3(rt_st, k2, _NT), 0.0)
        a_b = jnp.where(incl, _mm3(rt_st, b2, _NT), 0.0)
        t_inv = _unit_lower_inverse(m_b, c, 2 * n)
        v_st = stack(v)
        w1 = jnp.concatenate([_mm3(a_t[s * c:(s + 1) * c], s_scr[s, p], _NT) for s in range(ns)], axis=0)
        o1 = jnp.concatenate([_mm3(r_t[s * c:(s + 1) * c], s_scr[s, p], _NT) for s in range(ns)], axis=0)
        sa_st = _mm3(t_inv, stack(w1) + _mm3(m_k, v_st))
        o = o1 + unstack(_mm3(a_k, v_st) + _mm3(a_b, sa_st))
        sa = unstack(sa_st)
        o_ref[:, :, p * LANES:(p + 1) * LANES] = o.reshape(ns, c, LANES)[:, 0:tb, :]
        kg = k * g_end
        bg = beta * g_end
        for s in range(ns):
            rows = slice(s * c, (s + 1) * c)
            upd = _mm3(v[rows], kg[rows], _TN) + _mm3(sa[rows], bg[rows], _TN)
            decay = jnp.exp(cum_last[s * c:s * c + 1, :])
            s_scr[s, p] = s_scr[s, p] * decay + jnp.where(bd, upd, 0.0)

    @pl.when(j == pl.num_programs(1) - 1)
    def _():
        for s in range(ns):
            for p in range(pairs):
                blk = s_scr[s, p]
                s_out_ref[s, 2 * p] = blk[0:half, 0:half]
                s_out_ref[s, 2 * p + 1] = blk[half:LANES, half:LANES]


def _rwkv_chunk(r, ld, k, v, kk, a, s0):
    B, T, d = r.shape
    heads = d // RW_HEAD
    if T >= 64:
        c, ns, tb = 64, 1, 64
    else:
        c, ns, tb = 16, 4, T
    assert T % tb == 0 and B % ns == 0 and tb <= c
    tok = pl.BlockSpec((ns, tb, d), lambda i, j: (i, j, 0))
    st = pl.BlockSpec((ns, heads, RW_HEAD, RW_HEAD), lambda i, j: (i, 0, 0, 0))
    return pl.pallas_call(
        functools.partial(_rwkv_chunk_kernel, c=c),
        grid=(B // ns, T // tb),
        in_specs=[tok] * 6 + [st],
        out_specs=[tok, st],
        out_shape=[jax.ShapeDtypeStruct((B, T, d), F32),
                   jax.ShapeDtypeStruct((B, heads, RW_HEAD, RW_HEAD), F32)],
        scratch_shapes=[pltpu.VMEM((ns, d // LANES, LANES, LANES), F32)],
        compiler_params=_params(("parallel", "arbitrary")),
        name="rwkv_chunk",
    )(r, ld, k, v, kk, a, s0)


def _residual_out(y, w_ref, x_ref, mod_ref, fg_ref, o_ref, final):
    bb, tt, d = x_ref.shape
    out = _mm1(y, w_ref[...]).reshape(bb, tt, d)
    x = x_ref[...] + mod_ref[:, :, 2 * d:3 * d] * out
    if final:
        x = x * lax.rsqrt(jnp.mean(x * x, axis=-1, keepdims=True) + NORM_EPS) * fg_ref[...]
    o_ref[...] = x


def _rwkv_post_kernel(o_ref, bonus_ref, g_ref, x_ref, mod_ref, gnw_ref, gnb_ref, wo_ref, fg_ref, out_ref, *, final):
    bb, tt, d = x_ref.shape
    m = bb * tt
    o = o_ref[...].reshape(m, d)
    mu = _head_bcast(_head_sum(o, RW_HEAD) * (1.0 / RW_HEAD), d, RW_HEAD)
    oc = o - mu
    var = _head_bcast(_head_sum(oc * oc, RW_HEAD) * (1.0 / RW_HEAD), d, RW_HEAD)
    y = oc * lax.rsqrt(var + RW_GN_EPS) * gnw_ref[...] + gnb_ref[...]
    y = (y + bonus_ref[...].reshape(m, d)) * g_ref[...].reshape(m, d)
    _residual_out(y, wo_ref, x_ref, mod_ref, fg_ref, out_ref, final)


def _post_call(kernel, name, x, mod, tok_inputs, consts, final):
    B, T, d = x.shape
    bb, tt = _row_tiles(B, T)
    tok = lambda a: pl.BlockSpec((bb, tt, a.shape[-1]), lambda i, j: (i, j, 0))
    return pl.pallas_call(
        functools.partial(kernel, final=final),
        grid=(B // bb, T // tt),
        in_specs=[tok(a) for a in tok_inputs] + [tok(x), pl.BlockSpec((bb, 1, 3 * d), lambda i, j: (i, 0, 0))]
        + [_const_spec(c.shape) for c in consts],
        out_specs=tok(x),
        out_shape=jax.ShapeDtypeStruct((B, T, d), F32),
        compiler_params=_params(("parallel", "parallel")),
        name=name,
    )(*tok_inputs, x, mod, *consts)


def _rwkv_layer(x, mod, norm_g, final_g, final, x_prev, s0, p):
    d = x.shape[-1]
    r, ld, k, v, kk, a, g, bonus, shift = _rwkv_pre(x, x_prev, mod, norm_g, p)
    o, s_new = _rwkv_chunk(r, ld, k, v, kk, a, s0)
    consts = [p['gn_w'].reshape(1, d), p['gn_b'].reshape(1, d), p['w_o'].astype(BF16), final_g.reshape(1, d)]
    x = _post_call(_rwkv_post_kernel, "rwkv_post", x, mod, [o, bonus, g], consts, final)
    return x, s_new, shift[:, 0, :]


def _gla_pre_kernel(x_ref, mod_ref, ng_ref, w_ref, wl_ref, w2_ref, b2_ref, q_out, k_out, v_out, gate_out, gk_out,
                    *, qk):
    bb, tt, d = x_ref.shape
    m = bb * tt
    h = _prenorm(x_ref[...], ng_ref[...], mod_ref[...], d).reshape(m, d).astype(BF16)
    dk = qk // GLA_HEADS
    q_out[...] = (_dot(h, w_ref[:, 0:qk]) * dk ** -0.5).reshape(bb, tt, qk)
    k_out[...] = _dot(h, w_ref[:, qk:2 * qk]).reshape(bb, tt, qk)
    v_out[...] = _dot(h, w_ref[:, 2 * qk:2 * qk + d]).reshape(bb, tt, d)
    gate_out[...] = _dot(h, w_ref[:, 2 * qk + d:2 * qk + 2 * d]).reshape(bb, tt, d)
    low = _dot(h, wl_ref[...])
    gk = _mm3(low, w2_ref[...]) + b2_ref[...]
    gk_out[...] = (-_softplus(-gk) * (1.0 / GLA_GATE_NORMALIZER)).reshape(bb, tt, qk)


def _hg_pre_kernel(x_ref, mod_ref, ng_ref, w_ref, lower_ref, q_out, k_out, v_out, gate_out, g_out, *, layer):
    bb, tt, d = x_ref.shape
    m = bb * tt
    h = _prenorm(x_ref[...], ng_ref[...], mod_ref[...], d).reshape(m, d).astype(BF16)
    dk = d // HG_HEADS
    low = lower_ref[...]
    e = jnp.exp(low - jnp.max(low, axis=0, keepdims=True))
    soft = e / jnp.sum(e, axis=0, keepdims=True)
    lb = jnp.sum(soft[0:layer + 1], axis=0, keepdims=True) - soft[0:1]
    q = _dot(h, w_ref[:, 0:d])
    f = _dot(h, w_ref[:, d:2 * d])
    forget = lb + (1.0 - lb) * _sigmoid(f)
    q_out[...] = (_silu(q) * dk ** -0.5).reshape(bb, tt, d)
    k_out[...] = (1.0 - forget).reshape(bb, tt, d)
    g_out[...] = jnp.log(forget).reshape(bb, tt, d)
    v_out[...] = _dot(h, w_ref[:, 2 * d:3 * d]).reshape(bb, tt, d)
    gate_out[...] = _dot(h, w_ref[:, 3 * d:4 * d]).reshape(bb, tt, d)


def _pre_call(kernel, name, x, mod, consts, out_widths):
    B, T, d = x.shape
    bb, tt = _row_tiles(B, T)
    tok = lambda n: pl.BlockSpec((bb, tt, n), lambda i, j: (i, j, 0))
    return pl.pallas_call(
        kernel,
        grid=(B // bb, T // tt),
        in_specs=[tok(d), pl.BlockSpec((bb, 1, 3 * d), lambda i, j: (i, 0, 0))]
        + [_const_spec(c.shape) for c in consts],
        out_specs=[tok(n) for n in out_widths],
        out_shape=[jax.ShapeDtypeStruct((B, T, n), F32) for n in out_widths],
        compiler_params=_params(("parallel", "parallel")),
        name=name,
    )(x, mod, *consts)


def _gla_chunk_kernel(q_ref, k_ref, g_ref, v_ref, s0_ref, o_ref, s_out_ref, st_scr, *, c):
    tb = q_ref.shape[1]
    j = pl.program_id(2)

    @pl.when(j == 0)
    def _():
        st_scr[...] = s0_ref[0, 0].T

    def load(ref):
        x = ref[0]
        if tb < c:
            x = jnp.concatenate([x, jnp.zeros((c - tb, x.shape[1]), F32)], axis=0)
        return x

    q, k, g, v = load(q_ref), load(k_ref), load(g_ref), load(v_ref)
    tc = _iota((c, c), 0)
    sc = _iota((c, c), 1)
    b = _sel_mm3(jnp.where(sc <= tc, 1.0, 0.0).astype(BF16), g)
    st = st_scr[...]
    o_inter = _mm3(q * jnp.exp(b), st, _NT)
    sub = min(SUB, c)
    key_row = _iota((c, 1), 0)
    row = _iota((sub, 1), 0)
    outs = []
    for i in range(c // sub):
        r0 = i * sub
        qi = q[r0:r0 + sub]
        bi = b[r0:r0 + sub]
        acc = o_inter[r0:r0 + sub]
        if i > 0:
            ref_b = b[r0 - 1:r0]
            k_fac = jnp.exp(jnp.where(key_row < r0, ref_b - b, -jnp.inf))
            att = _mm3(qi * jnp.exp(bi - ref_b), k * k_fac, _NT)
            acc = acc + _mm3(att, v)
        for s in range(sub):
            diff = jnp.where(row >= s, bi - bi[s:s + 1], -jnp.inf)
            col = jnp.sum(qi * k[r0 + s:r0 + s + 1] * jnp.exp(diff), axis=1, keepdims=True)
            acc = acc + col * v[r0 + s:r0 + s + 1]
        outs.append(acc)
    o = jnp.concatenate(outs, axis=0) if len(outs) > 1 else outs[0]
    o_ref[0] = o[0:tb]
    b_last = b[c - 1:c]
    st_new = st * jnp.exp(b_last) + _mm3(v, k * jnp.exp(b_last - b), _TN)
    st_scr[...] = st_new

    @pl.when(j == pl.num_programs(2) - 1)
    def _():
        s_out_ref[0, 0] = st_new.T


def _gla_chunk(q, k, g, v, s0, heads):
    B, T, qk = q.shape
    dk = qk // heads
    dv = v.shape[-1] // heads
    tb = 64 if T % 64 == 0 else T
    c = -(-tb // SUB) * SUB
    kspec = pl.BlockSpec((1, tb, dk), lambda b, h, j: (b, j, h))
    vspec = pl.BlockSpec((1, tb, dv), lambda b, h, j: (b, j, h))
    sspec = pl.BlockSpec((1, 1, dk, dv), lambda b, h, j: (b, h, 0, 0))
    return pl.pallas_call(
        functools.partial(_gla_chunk_kernel, c=c),
        grid=(B, heads, T // tb),
        in_specs=[kspec, kspec, kspec, vspec, sspec],
        out_specs=[vspec, sspec],
        out_shape=[jax.ShapeDtypeStruct((B, T, heads * dv), F32),
                   jax.ShapeDtypeStruct((B, heads, dk, dv), F32)],
        scratch_shapes=[pltpu.VMEM((dv, dk), F32)],
        compiler_params=_params(("parallel", "parallel", "arbitrary")),
        name="gla_chunk",
    )(q, k, g, v, s0)


def _gla_post_kernel(o_ref, gate_ref, x_ref, mod_ref, gnw_ref, wo_ref, fg_ref, out_ref, *, final, heads):
    bb, tt, d = x_ref.shape
    m = bb * tt
    hs = d // heads
    o = o_ref[...].reshape(m, d)
    ms = _head_bcast(_head_sum(o * o, hs) * (1.0 / hs), d, hs)
    y = o * lax.rsqrt(ms + NORM_EPS) * gnw_ref[...]
    y = y * _silu(gate_ref[...].reshape(m, d))
    _residual_out(y, wo_ref, x_ref, mod_ref, fg_ref, out_ref, final)


def _gla_layer(x, mod, norm_g, final_g, final, s0, p):
    d = x.shape[-1]
    qk = p['w_gk2'].shape[1]
    w_in = p['w_in']
    consts = [norm_g.reshape(1, d), w_in[:, :2 * qk + 2 * d].astype(BF16),
              _pad_cols(w_in[:, 2 * qk + 2 * d:], LANES).astype(BF16), _pad_rows(p['w_gk2'], LANES),
              p['b_gk2'].reshape(1, qk)]
    q, k, v, gate, gk = _pre_call(functools.partial(_gla_pre_kernel, qk=qk), "gla_pre", x, mod, consts,
                                  (qk, qk, d, d, qk))
    o, s_new = _gla_chunk(q, k, gk, v, s0, GLA_HEADS)
    post_consts = [jnp.tile(p['gn_w'], GLA_HEADS).reshape(1, d), p['w_o'].astype(BF16), final_g.reshape(1, d)]
    x = _post_call(functools.partial(_gla_post_kernel, heads=GLA_HEADS), "gla_post", x, mod, [o, gate], post_consts,
                   final)
    return x, s_new


def _hg_layer(x, mod, norm_g, final_g, final, s0, layer, p):
    d = x.shape[-1]
    consts = [norm_g.reshape(1, d), p['w_in'].astype(BF16), p['lower']]
    q, k, v, gate, g = _pre_call(functools.partial(_hg_pre_kernel, layer=layer), "hgrn_pre", x, mod, consts,
                                 (d, d, d, d, d))
    o, s_new = _gla_chunk(q, k, g, v, s0, HG_HEADS)
    post_consts = [jnp.tile(p['gn_w'], HG_HEADS).reshape(1, d), p['w_o'].astype(BF16), final_g.reshape(1, d)]
    x = _post_call(functools.partial(_gla_post_kernel, heads=HG_HEADS), "hgrn_post", x, mod, [o, gate], post_consts,
                   final)
    return x, s_new


def _sb_pre_kernel(x_ref, mod_ref, ng_ref, w_ref, q_out, k_out, v_out, gate_out):
    bb, tt, d = x_ref.shape
    m = bb * tt
    h = _prenorm(x_ref[...], ng_ref[...], mod_ref[...], d).reshape(m, d).astype(BF16)
    hd = d // SB_HEADS
    q_out[...] = (_dot(h, w_ref[:, 0:d]) * hd ** -0.5).reshape(bb, tt, d)
    k_out[...] = _dot(h, w_ref[:, d:2 * d]).reshape(bb, tt, d)
    v_out[...] = _dot(h, w_ref[:, 2 * d:3 * d]).reshape(bb, tt, d)
    gate_out[...] = _dot(h, w_ref[:, 3 * d:4 * d]).reshape(bb, tt, d)


def _sb_block(z, before, carry, tri):
    l = -_softplus(z)
    lm = l if before is None else jnp.where(before, l, 0.0)
    cs = _mm_sel(lm, tri)
    logw = (z + l) + (cs - lm) + carry
    w = jnp.exp(logw)
    if before is not None:
        w = jnp.where(before, w, 0.0)
    return w, carry + cs[:, 0:1]


def _sb_prompt_kernel(bias_ref, q_ref, k_ref, v_ref, o_ref, *, tq):
    p = pl.program_id(1)
    i = pl.program_id(2)
    half = LANES // 2
    q = q_ref[0]
    lane_a = _iota((tq, LANES), 1) < half
    q_st = jnp.concatenate([jnp.where(lane_a, q, 0.0), jnp.where(lane_a, 0.0, q)], axis=0)
    qh, ql = _split(q_st)
    bias = jnp.where(_iota((2 * tq, 1), 0) < tq, bias_ref[2 * p], bias_ref[2 * p + 1])
    tri = jnp.where(_iota((tq, tq), 0) >= _iota((tq, tq), 1), 1.0, 0.0).astype(BF16)

    def logits(kb):
        kh, kl = _split(kb)
        return _dot(qh, kh, _NT) + (_dot(qh, kl, _NT) + _dot(ql, kh, _NT)) + bias

    kd = k_ref[0, pl.ds(pl.multiple_of(i * tq, tq), tq), :]
    vd = v_ref[0, pl.ds(pl.multiple_of(i * tq, tq), tq), :]
    before = _iota((2 * tq, tq), 1) < _iota((2 * tq, tq), 0) % tq
    w, carry = _sb_block(logits(kd), before, jnp.zeros((2 * tq, 1), F32), tri)
    acc = _mm1(w, vd)

    def body(n, state):
        acc, carry = state
        jb = i - 1 - n
        kb = k_ref[0, pl.ds(pl.multiple_of(jb * tq, tq), tq), :]
        vb = v_ref[0, pl.ds(pl.multiple_of(jb * tq, tq), tq), :]
        w, carry = _sb_block(logits(kb), None, carry, tri)
        return acc + _mm1(w, vb), carry

    acc, _ = lax.fori_loop(0, i, body, (acc, carry))
    o_ref[0] = jnp.where(lane_a, acc[0:tq], acc[tq:2 * tq])


def _sb_prompt(q, k, v, bias):
    B, T, d = q.shape
    tq = min(T, 256)
    qspec = pl.BlockSpec((1, tq, LANES), lambda b, p, i: (b, i, p))
    kspec = pl.BlockSpec((1, T, LANES), lambda b, p, i: (b, 0, p))
    return pl.pallas_call(
        functools.partial(_sb_prompt_kernel, tq=tq),
        grid=(B, d // LANES, T // tq),
        in_specs=[pl.BlockSpec(memory_space=pltpu.SMEM), qspec, kspec, kspec],
        out_specs=qspec,
        out_shape=jax.ShapeDtypeStruct((B, T, d), F32),
        compiler_params=_params(("parallel", "parallel", "arbitrary")),
        name="sb_attention_prompt",
    )(bias, q, k, v)


def _sb_paged_kernel(pt_ref, q_ref, kn_ref, vn_ref, bias_ref, kc_ref, vc_ref, o_ref, acc_scr, carry_scr, *, page):
    t, d = q_ref.shape[1], q_ref.shape[2]
    heads = LANES // t
    hd = d // heads
    j = pl.program_id(1)
    q = q_ref[0]
    qrows = jnp.concatenate([q] * heads, axis=0)
    head_of_row = _iota((LANES, d), 0) // t
    qm = jnp.where(_iota((LANES, d), 1) // hd == head_of_row, qrows, 0.0).astype(BF16)
    bias = bias_ref[...]

    def block(kb, vb, before, carry, nk):
        z = _dot(kb.astype(BF16), qm, _NT) + bias
        l = -_softplus(z)
        lm = l if before is None else jnp.where(before, l, 0.0)
        upper = jnp.where(_iota((nk, nk), 1) >= _iota((nk, nk), 0), 1.0, 0.0).astype(BF16)
        cs = _sel_mm(upper, lm)
        w = jnp.exp((z + l) + (cs - lm) + carry)
        if before is not None:
            w = jnp.where(before, w, 0.0)
        return _dot(w.astype(BF16), vb.astype(BF16), _TN), carry + cs[0:1, :]

    @pl.when(j == 0)
    def _():
        nk = 2 * t
        pad = jnp.zeros((nk - t, d), F32)
        before = _iota((nk, LANES), 0) < _iota((nk, LANES), 1) % t
        acc, carry = block(jnp.concatenate([kn_ref[0], pad], axis=0), jnp.concatenate([vn_ref[0], pad], axis=0),
                           before, jnp.zeros((1, LANES), F32), nk)
        acc_scr[...] = acc
        carry_scr[0:1, :] = carry

    acc, carry = block(kc_ref[0], vc_ref[0], None, carry_scr[0:1, :], page)
    acc_scr[...] += acc
    carry_scr[0:1, :] = carry

    @pl.when(j == pl.num_programs(1) - 1)
    def _():
        full = acc_scr[...]
        lane_head = _iota((t, d), 1) // hd
        out = jnp.zeros((t, d), F32)
        for h in range(heads):
            out = jnp.where(lane_head == h, full[h * t:(h + 1) * t, :], out)
        o_ref[0] = out


def _sb_paged(q, k_new, v_new, bias, k_cache, v_cache, page_table):
    B, T, d = q.shape
    n_pool, page = k_cache.shape[0], k_cache.shape[1]
    n_pages = page_table.shape[1]
    heads = SB_HEADS
    assert heads * T == LANES
    tok = pl.BlockSpec((1, T, d), lambda b, j, pt: (b, 0, 0))
    cache = pl.BlockSpec((1, page, d), lambda b, j, pt: (pt[b, n_pages - 1 - j], 0, 0))
    bias_cols = jnp.repeat(bias, T).reshape(1, LANES)
    grid_spec = pltpu.PrefetchScalarGridSpec(
        num_scalar_prefetch=1,
        grid=(B, n_pages),
        in_specs=[tok, tok, tok, pl.BlockSpec((1, LANES), lambda b, j, pt: (0, 0)), cache, cache],
        out_specs=tok,
        scratch_shapes=[pltpu.VMEM((LANES, d), F32), pltpu.VMEM((8, LANES), F32)],
    )
    return pl.pallas_call(
        functools.partial(_sb_paged_kernel, page=page),
        grid_spec=grid_spec,
        out_shape=jax.ShapeDtypeStruct((B, T, d), F32),
        compiler_params=_params(("parallel", "arbitrary")),
        name="sb_attention_paged",
    )(page_table, q, k_new, v_new, bias_cols, k_cache.reshape(n_pool, page, d), v_cache.reshape(n_pool, page, d))


def _sb_post_kernel(o_ref, gate_ref, x_ref, mod_ref, wo_ref, fg_ref, out_ref, *, final):
    bb, tt, d = x_ref.shape
    m = bb * tt
    y = o_ref[...].reshape(m, d) * _silu(gate_ref[...].reshape(m, d))
    _residual_out(y, wo_ref, x_ref, mod_ref, fg_ref, out_ref, final)


def _sb_layer(x, mod, norm_g, final_g, final, past, p):
    B, T, d = x.shape
    consts = [norm_g.reshape(1, d), p['w_in'].astype(BF16)]
    q, k, v, gate = _pre_call(_sb_pre_kernel, "sb_pre", x, mod, consts, (d, d, d, d))
    if past is None:
        o = _sb_prompt(q, k, v, p['bias'])
    else:
        o = _sb_paged(q, k, v, p['bias'], past[0], past[1], past[2])
    x = _post_call(_sb_post_kernel, "sb_post", x, mod, [o, gate], [p['w_o'].astype(BF16), final_g.reshape(1, d)],
                   final)
    hd = d // SB_HEADS
    return x, k.reshape(B, T, SB_HEADS, hd), v.reshape(B, T, SB_HEADS, hd)


def kernel(x_prompt, x_sample, c_prompt, c_sample, state_rwkv, cache_rwkv_shift, state_gla, cache_sb_k, cache_sb_v, state_hgrn, page_table, norm_g, ada_w, ada_b, final_g, rw_mix, rw_w_rkvg, rw_w0, rw_w1, rw_w2, rw_a0, rw_a1, rw_a2, rw_k_k, rw_k_a, rw_r_k, rw_gn_w, rw_gn_b, rw_w_o, gla_w_in, gla_w_gk2, gla_b_gk2, gla_gn_w, gla_w_o, sb_w_in, sb_bias, sb_w_o, hg_w_in, hg_lower, hg_gn_w, hg_w_o):
    depth, d = norm_g.shape
    n_mix = 4
    bp, bs = x_prompt.shape[0], x_sample.shape[0]
    rows = -(-(bp + bs) // 8) * 8
    c_all = jnp.pad(jnp.concatenate([c_prompt, c_sample], axis=0), ((0, rows - bp - bs), (0, 0)))
    mods = _modulation(c_all, ada_w, ada_b)

    def trunk(x, mod_all, rw_s, rw_shift, gla_s, sb_past, hg_s):
        B = x.shape[0]
        outs = {n: [] for n in ('rw_s', 'rw_shift', 'gla_s', 'sb_k', 'sb_v', 'hg_s')}
        for i in range(depth):
            kind, j = i % n_mix, i // n_mix
            mod = mod_all[i].reshape(B, 1, 3 * d)
            final = i == depth - 1
            if kind == 0:
                p = dict(mix=rw_mix[j], w_rkvg=rw_w_rkvg[j], w0=rw_w0[j], w1=rw_w1[j], w2=rw_w2[j], a0=rw_a0[j],
                         a1=rw_a1[j], a2=rw_a2[j], k_k=rw_k_k[j], k_a=rw_k_a[j], r_k=rw_r_k[j], gn_w=rw_gn_w[j],
                         gn_b=rw_gn_b[j], w_o=rw_w_o[j])
                x, s, sh = _rwkv_layer(x, mod, norm_g[i], final_g, final, rw_shift[j], rw_s[j], p)
                outs['rw_s'].append(s)
                outs['rw_shift'].append(sh)
            elif kind == 1:
                p = dict(w_in=gla_w_in[j], w_gk2=gla_w_gk2[j], b_gk2=gla_b_gk2[j], gn_w=gla_gn_w[j], w_o=gla_w_o[j])
                x, s = _gla_layer(x, mod, norm_g[i], final_g, final, gla_s[j], p)
                outs['gla_s'].append(s)
            elif kind == 2:
                p = dict(w_in=sb_w_in[j], bias=sb_bias[j], w_o=sb_w_o[j])
                x, k_new, v_new = _sb_layer(x, mod, norm_g[i], final_g, final, sb_past(j), p)
                outs['sb_k'].append(k_new)
                outs['sb_v'].append(v_new)
            else:
                p = dict(w_in=hg_w_in[j], lower=hg_lower, gn_w=hg_gn_w[j], w_o=hg_w_o[j])
                x, s = _hg_layer(x, mod, norm_g[i], final_g, final, hg_s[j], i, p)
                outs['hg_s'].append(s)
        st = lambda n: jnp.stack(outs[n])
        return x, st('rw_s'), st('rw_shift'), st('gla_s'), st('sb_k'), st('sb_v'), st('hg_s')

    z = lambda a, b: jnp.zeros((a.shape[0], b) + a.shape[2:], a.dtype)
    prompt = trunk(x_prompt, mods[:, :bp], z(state_rwkv, bp), z(cache_rwkv_shift, bp), z(state_gla, bp),
                   lambda j: None, z(state_hgrn, bp))
    sample = trunk(x_sample, mods[:, bp:bp + bs], state_rwkv, cache_rwkv_shift, state_gla,
                   lambda j: (cache_sb_k[j], cache_sb_v[j], page_table), state_hgrn)
    return (prompt[0], sample[0]) + prompt[1:] + sample[1:]
```

```python
import functools
import math

import jax
import jax.numpy as jnp
from jax import lax
from jax.experimental import pallas as pl
from jax.experimental.pallas import tpu as pltpu

F32 = jnp.float32
BF16 = jnp.bfloat16

NORM_EPS = 1e-6
RW_HEAD = 64
RW_GN_EPS = 64e-5
GLA_HEADS = 4
GLA_GATE_NORMALIZER = 16.0
SB_HEADS = 16
HG_HEADS = 8
LANES = 128
ROW_TILE = 512
SUB = 16
VMEM_LIMIT = 56 * 1024 * 1024
LOG2E = 1.4426950408889634

_NN = (((1,), (0,)), ((), ()))
_NT = (((1,), (1,)), ((), ()))
_TN = (((0,), (0,)), ((), ()))
_BNN = (((2,), (1,)), ((0,), (0,)))
_BNT = (((2,), (2,)), ((0,), (0,)))
_BTN = (((1,), (1,)), ((0,), (0,)))


def _dot(a, b, dims=_NN):
    return lax.dot_general(a, b, dims, preferred_element_type=F32)


def _split(x):
    hi = x.astype(BF16)
    lo = (x - hi.astype(F32)).astype(BF16)
    return hi, lo


def _mm1(a, b, dims=_NN):
    return _dot(a.astype(BF16), b.astype(BF16), dims)


def _mm3(a, b, dims=_NN):
    ah, al = _split(a)
    bh, bl = _split(b)
    return _dot(ah, bh, dims) + (_dot(ah, bl, dims) + _dot(al, bh, dims))


def _mm_sel(a, sel, dims=_NN):
    ah, al = _split(a)
    return _dot(ah, sel, dims) + _dot(al, sel, dims)


def _mm_sel_fused(a, sel2):
    ah, al = _split(a)
    return _dot(jnp.concatenate([ah, al], axis=1), sel2)


def _sel_mm(sel, b, dims=_NN):
    bh, bl = _split(b)
    return _dot(sel, bh, dims) + _dot(sel, bl, dims)


def _sel_mm3(sel, b):
    b1 = b.astype(BF16)
    r1 = b - b1.astype(F32)
    b2 = r1.astype(BF16)
    b3 = (r1 - b2.astype(F32)).astype(BF16)
    return _dot(sel, b1) + (_dot(sel, b2) + _dot(sel, b3))


def _silu(x):
    return x * (1.0 / (1.0 + jnp.exp(-x)))


def _sigmoid(x):
    return 1.0 / (1.0 + jnp.exp(-x))


def _softplus(x):
    return jnp.maximum(x, 0.0) + jnp.log1p(jnp.exp(-jnp.abs(x)))


def _iota(shape, dim):
    return lax.broadcasted_iota(jnp.int32, shape, dim)


def _head_indicator(d, hs):
    return jnp.where(_iota((d, LANES), 0) // hs == _iota((d, LANES), 1), 1.0, 0.0).astype(BF16)


def _head_indicator_t(d, hs):
    return jnp.where(_iota((LANES, d), 1) // hs == _iota((LANES, d), 0), 1.0, 0.0).astype(BF16)


def _head_sum(x, hs):
    return _mm_sel(x, _head_indicator(x.shape[-1], hs))


def _head_bcast(s, d, hs):
    return _mm_sel(s, _head_indicator_t(d, hs))


def _row_tiles(B, T):
    tt = min(T, ROW_TILE)
    bb = max(1, min(B, ROW_TILE // tt))
    assert T % tt == 0 and B % bb == 0 and tt % 8 == 0
    return bb, tt


def _params(sem):
    return pltpu.CompilerParams(dimension_semantics=sem, vmem_limit_bytes=VMEM_LIMIT)


def _const_spec(shape):
    n = len(shape)
    return pl.BlockSpec(shape, lambda *_: (0,) * n)


def _prenorm(x, g, mod, d):
    shift = mod[:, :, 0:d]
    scale = mod[:, :, d:2 * d]
    y = x * lax.rsqrt(jnp.mean(x * x, axis=-1, keepdims=True) + NORM_EPS)
    return (y * g) * (1.0 + scale) + shift


def _mod_kernel(c_ref, w_ref, b_ref, o_ref):
    o_ref[0] = _mm1(_silu(c_ref[...]), w_ref[0]) + b_ref[0]


def _modulation(c, ada_w, ada_b):
    depth, d, n = ada_w.shape
    rows = c.shape[0]
    tn = 1536
    return pl.pallas_call(
        _mod_kernel,
        grid=(depth, n // tn),
        in_specs=[pl.BlockSpec((rows, d), lambda l, j: (0, 0)),
                  pl.BlockSpec((1, d, tn), lambda l, j: (l, 0, j)),
                  pl.BlockSpec((1, 1, tn), lambda l, j: (l, 0, j))],
        out_specs=pl.BlockSpec((1, rows, tn), lambda l, j: (l, 0, j)),
        out_shape=jax.ShapeDtypeStruct((depth, rows, n), F32),
        compiler_params=_params(("parallel", "parallel")),
        name="ada_modulation",
    )(c, ada_w, ada_b.reshape(depth, 1, n))


def _rwkv_pre_kernel(x_ref, halo_ref, xprev_ref, mod_ref, ng_ref, mix_ref, wr_ref, wk_ref, wv_ref, wg_ref,
                     w0_ref, w1_ref, w2_ref, a0_ref, a1_ref, a2_ref, kk_ref, ka_ref, rk_ref,
                     r_out, ld_out, k_out, v_out, kkn_out, a_out, g_out, bonus_out, shift_out):
    bb, tt, d = x_ref.shape
    m = bb * tt
    mod = mod_ref[...]
    g = ng_ref[...]
    h3 = _prenorm(x_ref[...], g, mod, d)
    h_halo = _prenorm(halo_ref[:, 7:8, :], g, mod, d)
    h_prev = jnp.where(pl.program_id(1) == 0, xprev_ref[...], h_halo)
    shift_out[...] = h3[:, tt - 1:tt, :]
    h = h3.reshape(m, d)
    prev = jnp.where(_iota((m, 1), 0) % tt == 0,
                     jnp.broadcast_to(h_prev, (bb, tt, d)).reshape(m, d),
                     pltpu.roll(h, 1, axis=0))
    xx = prev - h
    xr, xw, xk, xv, xa, xg = (h + xx * mix_ref[n:n + 1, :] for n in range(6))
    r = _mm1(xr, wr_ref[...])
    k = _mm1(xk, wk_ref[...])
    v = _mm1(xv, wv_ref[...])
    gate = _silu(_mm1(xg, wg_ref[...]))
    lw = w0_ref[...] + _mm1(jnp.tanh(_mm1(xw, w1_ref[...])), w2_ref[...])
    log_w = -_softplus(-lw) - 0.5
    a = _sigmoid(a0_ref[...] + _mm1(_mm1(xa, a1_ref[...]), a2_ref[...]))
    kk = k * kk_ref[...]
    nrm = jnp.maximum(jnp.sqrt(_head_sum(kk * kk, RW_HEAD)), 1e-12)
    kk = kk * _head_bcast(1.0 / nrm, d, RW_HEAD)
    k = k * (1.0 + (a - 1.0) * ka_ref[...])
    bonus = _head_bcast(_head_sum(r * k * rk_ref[...], RW_HEAD), d, RW_HEAD) * v
    for ref, val in ((r_out, r), (ld_out, -jnp.exp(log_w)), (k_out, k), (v_out, v), (kkn_out, kk),
                     (a_out, a), (g_out, gate), (bonus_out, bonus)):
        ref[...] = val.reshape(bb, tt, d)


def _pad_cols(w, n):
    return jnp.pad(w, ((0, 0), (0, n - w.shape[1])))


def _pad_rows(w, n):
    return jnp.pad(w, ((0, n - w.shape[0]), (0, 0)))


def _rwkv_pre(x, x_prev, mod, norm_g, p):
    B, T, d = x.shape
    tt = min(T, ROW_TILE // 2)
    bb = max(1, min(B, (ROW_TILE // 2) // tt))
    row = lambda a: a.reshape(1, d)
    tok = pl.BlockSpec((bb, tt, d), lambda i, j: (i, j, 0))
    per_seq = lambda n: pl.BlockSpec((bb, 1, n), lambda i, j: (i, 0, 0))
    halo = pl.BlockSpec((bb, 8, d), lambda i, j: (i, jnp.maximum(j * (tt // 8) - 1, 0), 0))
    consts = [row(norm_g), p['mix'],
              p['w_rkvg'][0].astype(BF16), p['w_rkvg'][1].astype(BF16), p['w_rkvg'][2].astype(BF16),
              p['w_rkvg'][3].astype(BF16),
              row(p['w0']), _pad_cols(p['w1'], LANES).astype(BF16), _pad_rows(p['w2'], LANES).astype(BF16),
              row(p['a0']), _pad_cols(p['a1'], LANES).astype(BF16), _pad_rows(p['a2'], LANES).astype(BF16),
              row(p['k_k']), row(p['k_a']), row(p['r_k'])]
    out = pl.pallas_call(
        _rwkv_pre_kernel,
        grid=(B // bb, T // tt),
        in_specs=[tok, halo, per_seq(d), per_seq(3 * d)] + [_const_spec(c.shape) for c in consts],
        out_specs=[tok] * 8 + [per_seq(d)],
        out_shape=[jax.ShapeDtypeStruct((B, T, d), F32)] * 8 + [jax.ShapeDtypeStruct((B, 1, d), F32)],
        compiler_params=_params(("parallel", "arbitrary")),
        name="rwkv_pre",
    )(x, x, x_prev.reshape(B, 1, d), mod, *consts)
    return out


def _unit_lower_inverse(m_strict, c):
    n = m_strict.shape[-1]
    ti = _iota((n, n), 0)
    si = _iota((n, n), 1)
    eye = jnp.where(ti == si, 1.0, 0.0)
    sub = min(SUB, c)
    d_part = jnp.where(ti // sub == si // sub, m_strict, 0.0)
    t_d = eye + d_part
    pw = d_part
    for _ in range(int(math.log2(sub)) - 1):
        pw = _mm3(pw, pw, _BNN)
        t_d = t_d + _mm3(t_d, pw, _BNN)
    if sub == c:
        return t_d
    assert c // sub == 4
    nn = _mm3(t_d, m_strict - d_part, _BNN)
    n2 = _mm3(nn, nn, _BNN)
    return _mm3(eye + nn + n2 + _mm3(nn, n2, _BNN), t_d, _BNN)


def _rwkv_chunk_kernel(r_ref, ld_ref, k_ref, v_ref, kk_ref, a_ref, s0_ref, o_ref, s_out_ref, s_scr, *, c):
    ns, tb, d = r_ref.shape
    n = ns * c
    pairs = d // LANES
    half = LANES // 2
    j = pl.program_id(1)

    @pl.when(j == 0)
    def _():
        z = jnp.zeros((half, half), F32)
        for s in range(ns):
            for p in range(pairs):
                top = jnp.concatenate([s0_ref[s, 2 * p], z], axis=1)
                bot = jnp.concatenate([z, s0_ref[s, 2 * p + 1]], axis=1)
                s_scr[s, p] = jnp.concatenate([top, bot], axis=0)

    lane_a = _iota((n, LANES), 1) < half
    ti = _iota((2 * n, 2 * n), 0)
    si = _iota((2 * n, 2 * n), 1)
    same = ti // c == si // c
    strict = same & (si < ti)
    incl = same & (si <= ti)
    tc = _iota((n, n), 0)
    sc = _iota((n, n), 1)
    tri = jnp.where((tc // c == sc // c) & (sc <= tc), 1.0, 0.0).astype(BF16)
    last = jnp.where((tc // c == sc // c) & (sc % c == c - 1), 1.0, 0.0).astype(BF16)
    bd = (_iota((LANES, LANES), 0) < half) == (_iota((LANES, LANES), 1) < half)

    def load(ref):
        x = ref[...]
        if tb < c:
            x = jnp.concatenate([x, jnp.zeros((ns, c - tb, d), F32)], axis=1)
        return x.reshape(n, d)

    def by_pair(x):
        return jnp.stack([x[:, p * LANES:(p + 1) * LANES] for p in range(pairs)], axis=0)

    def stack(x):
        return jnp.concatenate([jnp.where(lane_a, x, 0.0), jnp.where(lane_a, 0.0, x)], axis=1)

    def unstack(x):
        return x[:, 0:n] + x[:, n:2 * n]

    r, ld, k, v, kk, a = (load(ref) for ref in (r_ref, ld_ref, k_ref, v_ref, kk_ref, a_ref))
    cum = _sel_mm3(tri, ld)
    cum_last = _sel_mm3(last, cum)
    g_end = jnp.exp(cum_last - cum)
    g_inv = jnp.exp(-cum)
    beta = kk * a
    a_t = by_pair(-kk * jnp.exp(cum - ld))
    r_t = by_pair(r * jnp.exp(cum))
    k_t = by_pair(k * g_inv)
    b_t = by_pair(beta * g_inv)
    at_st = stack(a_t)
    rt_st = stack(r_t)
    k2 = jnp.concatenate([k_t, k_t], axis=1)
    b2 = jnp.concatenate([b_t, b_t], axis=1)
    m_k = jnp.where(strict, _mm3(at_st, k2, _BNT), 0.0)
    m_b = jnp.where(strict, _mm3(at_st, b2, _BNT), 0.0)
    a_k = jnp.where(incl, _mm3(rt_st, k2, _BNT), 0.0)
    a_b = jnp.where(incl, _mm3(rt_st, b2, _BNT), 0.0)
    t_inv = _unit_lower_inverse(m_b, c)
    v_p = by_pair(v)
    v_st = stack(v_p)
    states = [s_scr[s] for s in range(ns)]
    w1 = jnp.concatenate([_mm1(a_t[:, s * c:(s + 1) * c], states[s], _BNT) for s in range(ns)], axis=1)
    o1 = jnp.concatenate([_mm1(r_t[:, s * c:(s + 1) * c], states[s], _BNT) for s in range(ns)], axis=1)
    sa_st = _mm1(t_inv, stack(w1) + _mm1(m_k, v_st, _BNN), _BNN)
    o = o1 + unstack(_mm1(a_k, v_st, _BNN) + _mm1(a_b, sa_st, _BNN))
    sa = unstack(sa_st)
    for p in range(pairs):
        o_ref[:, :, p * LANES:(p + 1) * LANES] = o[p].reshape(ns, c, LANES)[:, 0:tb, :]
    kg = by_pair(k * g_end)
    bg = by_pair(beta * g_end)
    for s in range(ns):
        rows = slice(s * c, (s + 1) * c)
        upd = _mm1(v_p[:, rows], kg[:, rows], _BTN) + _mm1(sa[:, rows], bg[:, rows], _BTN)
        decay = by_pair(jnp.exp(cum_last[s * c:s * c + 1, :]))
        s_scr[s] = states[s] * decay + jnp.where(bd, upd, 0.0)

    @pl.when(j == pl.num_programs(1) - 1)
    def _():
        for s in range(ns):
            for p in range(pairs):
                blk = s_scr[s, p]
                s_out_ref[s, 2 * p] = blk[0:half, 0:half]
                s_out_ref[s, 2 * p + 1] = blk[half:LANES, half:LANES]


def _rwkv_chunk(r, ld, k, v, kk, a, s0):
    B, T, d = r.shape
    heads = d // RW_HEAD
    if T >= 64:
        c, ns, tb = 64, 1, 64
    else:
        c, ns, tb = 16, 4, T
    assert T % tb == 0 and B % ns == 0 and tb <= c
    tok = pl.BlockSpec((ns, tb, d), lambda i, j: (i, j, 0))
    st = pl.BlockSpec((ns, heads, RW_HEAD, RW_HEAD), lambda i, j: (i, 0, 0, 0))
    return pl.pallas_call(
        functools.partial(_rwkv_chunk_kernel, c=c),
        grid=(B // ns, T // tb),
        in_specs=[tok] * 6 + [st],
        out_specs=[tok, st],
        out_shape=[jax.ShapeDtypeStruct((B, T, d), F32),
                   jax.ShapeDtypeStruct((B, heads, RW_HEAD, RW_HEAD), F32)],
        scratch_shapes=[pltpu.VMEM((ns, d // LANES, LANES, LANES), F32)],
        compiler_params=_params(("parallel", "arbitrary")),
        name="rwkv_chunk",
    )(r, ld, k, v, kk, a, s0)


def _residual_out(y, w_ref, x_ref, mod_ref, fg_ref, o_ref, final):
    bb, tt, d = x_ref.shape
    out = _mm1(y, w_ref[...]).reshape(bb, tt, d)
    x = x_ref[...] + mod_ref[:, :, 2 * d:3 * d] * out
    if final:
        x = x * lax.rsqrt(jnp.mean(x * x, axis=-1, keepdims=True) + NORM_EPS) * fg_ref[...]
    o_ref[...] = x


def _rwkv_post_kernel(o_ref, bonus_ref, g_ref, x_ref, mod_ref, gnw_ref, gnb_ref, wo_ref, fg_ref, out_ref, *, final):
    bb, tt, d = x_ref.shape
    m = bb * tt
    o = o_ref[...].reshape(m, d)
    mu = _head_bcast(_head_sum(o, RW_HEAD) * (1.0 / RW_HEAD), d, RW_HEAD)
    oc = o - mu
    var = _head_bcast(_head_sum(oc * oc, RW_HEAD) * (1.0 / RW_HEAD), d, RW_HEAD)
    y = oc * lax.rsqrt(var + RW_GN_EPS) * gnw_ref[...] + gnb_ref[...]
    y = (y + bonus_ref[...].reshape(m, d)) * g_ref[...].reshape(m, d)
    _residual_out(y, wo_ref, x_ref, mod_ref, fg_ref, out_ref, final)


def _post_call(kernel, name, x, mod, tok_inputs, consts, final):
    B, T, d = x.shape
    bb, tt = _row_tiles(B, T)
    tok = lambda a: pl.BlockSpec((bb, tt, a.shape[-1]), lambda i, j: (i, j, 0))
    return pl.pallas_call(
        functools.partial(kernel, final=final),
        grid=(B // bb, T // tt),
        in_specs=[tok(a) for a in tok_inputs] + [tok(x), pl.BlockSpec((bb, 1, 3 * d), lambda i, j: (i, 0, 0))]
        + [_const_spec(c.shape) for c in consts],
        out_specs=tok(x),
        out_shape=jax.ShapeDtypeStruct((B, T, d), F32),
        compiler_params=_params(("parallel", "parallel")),
        name=name,
    )(*tok_inputs, x, mod, *consts)


def _rwkv_layer(x, mod, norm_g, final_g, final, x_prev, s0, p):
    d = x.shape[-1]
    r, ld, k, v, kk, a, g, bonus, shift = _rwkv_pre(x, x_prev, mod, norm_g, p)
    o, s_new = _rwkv_chunk(r, ld, k, v, kk, a, s0)
    consts = [p['gn_w'].reshape(1, d), p['gn_b'].reshape(1, d), p['w_o'].astype(BF16), final_g.reshape(1, d)]
    x = _post_call(_rwkv_post_kernel, "rwkv_post", x, mod, [o, bonus, g], consts, final)
    return x, s_new, shift[:, 0, :]


def _gla_pre_kernel(x_ref, mod_ref, ng_ref, w_ref, wl_ref, w2_ref, b2_ref, q_out, k_out, v_out, gate_out, gk_out,
                    *, qk):
    bb, tt, d = x_ref.shape
    m = bb * tt
    h = _prenorm(x_ref[...], ng_ref[...], mod_ref[...], d).reshape(m, d).astype(BF16)
    dk = qk // GLA_HEADS
    q_out[...] = (_dot(h, w_ref[:, 0:qk]) * dk ** -0.5).reshape(bb, tt, qk)
    k_out[...] = _dot(h, w_ref[:, qk:2 * qk]).reshape(bb, tt, qk)
    v_out[...] = _dot(h, w_ref[:, 2 * qk:2 * qk + d]).reshape(bb, tt, d)
    gate_out[...] = _dot(h, w_ref[:, 2 * qk + d:2 * qk + 2 * d]).reshape(bb, tt, d)
    low = _dot(h, wl_ref[...])
    gk = _mm3(low, w2_ref[...]) + b2_ref[...]
    gk_out[...] = (-_softplus(-gk) * (1.0 / GLA_GATE_NORMALIZER)).reshape(bb, tt, qk)


def _hg_pre_kernel(x_ref, mod_ref, ng_ref, w_ref, lower_ref, q_out, k_out, v_out, gate_out, g_out, *, layer):
    bb, tt, d = x_ref.shape
    m = bb * tt
    h = _prenorm(x_ref[...], ng_ref[...], mod_ref[...], d).reshape(m, d).astype(BF16)
    dk = d // HG_HEADS
    low = lower_ref[...]
    e = jnp.exp(low - jnp.max(low, axis=0, keepdims=True))
    soft = e / jnp.sum(e, axis=0, keepdims=True)
    lb = jnp.sum(soft[0:layer + 1], axis=0, keepdims=True) - soft[0:1]
    q = _dot(h, w_ref[:, 0:d])
    f = _dot(h, w_ref[:, d:2 * d])
    forget = lb + (1.0 - lb) * _sigmoid(f)
    q_out[...] = (_silu(q) * dk ** -0.5).reshape(bb, tt, d)
    k_out[...] = (1.0 - forget).reshape(bb, tt, d)
    g_out[...] = jnp.log(forget).reshape(bb, tt, d)
    v_out[...] = _dot(h, w_ref[:, 2 * d:3 * d]).reshape(bb, tt, d)
    gate_out[...] = _dot(h, w_ref[:, 3 * d:4 * d]).reshape(bb, tt, d)


def _pre_call(kernel, name, x, mod, consts, out_widths):
    B, T, d = x.shape
    bb, tt = _row_tiles(B, T)
    tok = lambda n: pl.BlockSpec((bb, tt, n), lambda i, j: (i, j, 0))
    return pl.pallas_call(
        kernel,
        grid=(B // bb, T // tt),
        in_specs=[tok(d), pl.BlockSpec((bb, 1, 3 * d), lambda i, j: (i, 0, 0))]
        + [_const_spec(c.shape) for c in consts],
        out_specs=[tok(n) for n in out_widths],
        out_shape=[jax.ShapeDtypeStruct((B, T, n), F32) for n in out_widths],
        compiler_params=_params(("parallel", "parallel")),
        name=name,
    )(x, mod, *consts)


def _gla_chunk_kernel(q_ref, k_ref, g_ref, v_ref, s0_ref, o_ref, s_out_ref, st_scr, *, c, heads):
    tb = q_ref.shape[1]
    dk = q_ref.shape[2] // heads
    dv = v_ref.shape[2] // heads
    j = pl.program_id(1)

    @pl.when(j == 0)
    def _():
        for h in range(heads):
            st_scr[h] = s0_ref[0, h].T

    def load(ref, w):
        x = ref[0]
        if tb < c:
            x = jnp.concatenate([x, jnp.zeros((c - tb, x.shape[1]), F32)], axis=0)
        return jnp.stack([x[:, h * w:(h + 1) * w] for h in range(heads)], axis=0)

    q, k, g, v = load(q_ref, dk), load(k_ref, dk), load(g_ref, dk), load(v_ref, dv)
    tc = _iota((c, c), 0)
    sc = _iota((c, c), 1)
    tri = jnp.where(sc <= tc, 1.0, 0.0).astype(BF16)
    b = jnp.stack([_sel_mm3(tri, g[h]) for h in range(heads)], axis=0)
    st = st_scr[...]
    o_inter = _mm3(q * jnp.exp(b), st, _BNT)
    sub = min(SUB, c)
    key_row = _iota((c, 1), 0)
    row = _iota((sub, 1), 0)
    outs = []
    for i in range(c // sub):
        r0 = i * sub
        qi = q[:, r0:r0 + sub]
        bi = b[:, r0:r0 + sub]
        acc = o_inter[:, r0:r0 + sub]
        if i > 0:
            ref_b = b[:, r0 - 1:r0]
            k_fac = jnp.exp(jnp.where(key_row < r0, ref_b - b, -jnp.inf))
            att = _mm3(qi * jnp.exp(bi - ref_b), k * k_fac, _BNT)
            acc = acc + _mm3(att, v, _BNN)
        for s in range(sub):
            diff = jnp.where(row >= s, bi - bi[:, s:s + 1], -jnp.inf)
            col = jnp.sum(qi * k[:, r0 + s:r0 + s + 1] * jnp.exp(diff), axis=2, keepdims=True)
            acc = acc + col * v[:, r0 + s:r0 + s + 1]
        outs.append(acc)
    o = jnp.concatenate(outs, axis=1) if len(outs) > 1 else outs[0]
    for h in range(heads):
        o_ref[0, :, h * dv:(h + 1) * dv] = o[h, 0:tb]
    b_last = b[:, c - 1:c]
    st_new = st * jnp.exp(b_last) + _mm3(v, k * jnp.exp(b_last - b), _BTN)
    st_scr[...] = st_new

    @pl.when(j == pl.num_programs(1) - 1)
    def _():
        for h in range(heads):
            s_out_ref[0, h] = st_new[h].T


def _gla_chunk(q, k, g, v, s0, heads):
    B, T, qk = q.shape
    d = v.shape[-1]
    dk = qk // heads
    dv = d // heads
    tb = 64 if T % 64 == 0 else T
    c = -(-tb // SUB) * SUB
    kspec = pl.BlockSpec((1, tb, qk), lambda b, j: (b, j, 0))
    vspec = pl.BlockSpec((1, tb, d), lambda b, j: (b, j, 0))
    sspec = pl.BlockSpec((1, heads, dk, dv), lambda b, j: (b, 0, 0, 0))
    return pl.pallas_call(
        functools.partial(_gla_chunk_kernel, c=c, heads=heads),
        grid=(B, T // tb),
        in_specs=[kspec, kspec, kspec, vspec, sspec],
        out_specs=[vspec, sspec],
        out_shape=[jax.ShapeDtypeStruct((B, T, d), F32),
                   jax.ShapeDtypeStruct((B, heads, dk, dv), F32)],
        scratch_shapes=[pltpu.VMEM((heads, dv, dk), F32)],
        compiler_params=_params(("parallel", "arbitrary")),
        name="gla_chunk",
    )(q, k, g, v, s0)


def _gla_post_kernel(o_ref, gate_ref, x_ref, mod_ref, gnw_ref, wo_ref, fg_ref, out_ref, *, final, heads):
    bb, tt, d = x_ref.shape
    m = bb * tt
    hs = d // heads
    o = o_ref[...].reshape(m, d)
    ms = _head_bcast(_head_sum(o * o, hs) * (1.0 / hs), d, hs)
    y = o * lax.rsqrt(ms + NORM_EPS) * gnw_ref[...]
    y = y * _silu(gate_ref[...].reshape(m, d))
    _residual_out(y, wo_ref, x_ref, mod_ref, fg_ref, out_ref, final)


def _gla_layer(x, mod, norm_g, final_g, final, s0, p):
    d = x.shape[-1]
    qk = p['w_gk2'].shape[1]
    w_in = p['w_in']
    consts = [norm_g.reshape(1, d), w_in[:, :2 * qk + 2 * d].astype(BF16),
              _pad_cols(w_in[:, 2 * qk + 2 * d:], LANES).astype(BF16), _pad_rows(p['w_gk2'], LANES),
              p['b_gk2'].reshape(1, qk)]
    q, k, v, gate, gk = _pre_call(functools.partial(_gla_pre_kernel, qk=qk), "gla_pre", x, mod, consts,
                                  (qk, qk, d, d, qk))
    o, s_new = _gla_chunk(q, k, gk, v, s0, GLA_HEADS)
    post_consts = [jnp.tile(p['gn_w'], GLA_HEADS).reshape(1, d), p['w_o'].astype(BF16), final_g.reshape(1, d)]
    x = _post_call(functools.partial(_gla_post_kernel, heads=GLA_HEADS), "gla_post", x, mod, [o, gate], post_consts,
                   final)
    return x, s_new


def _hg_layer(x, mod, norm_g, final_g, final, s0, layer, p):
    d = x.shape[-1]
    consts = [norm_g.reshape(1, d), p['w_in'].astype(BF16), p['lower']]
    q, k, v, gate, g = _pre_call(functools.partial(_hg_pre_kernel, layer=layer), "hgrn_pre", x, mod, consts,
                                 (d, d, d, d, d))
    o, s_new = _gla_chunk(q, k, g, v, s0, HG_HEADS)
    post_consts = [jnp.tile(p['gn_w'], HG_HEADS).reshape(1, d), p['w_o'].astype(BF16), final_g.reshape(1, d)]
    x = _post_call(functools.partial(_gla_post_kernel, heads=HG_HEADS), "hgrn_post", x, mod, [o, gate], post_consts,
                   final)
    return x, s_new


def _sb_pre_kernel(x_ref, mod_ref, ng_ref, w_ref, q_out, k_out, v_out, gate_out):
    bb, tt, d = x_ref.shape
    m = bb * tt
    h = _prenorm(x_ref[...], ng_ref[...], mod_ref[...], d).reshape(m, d).astype(BF16)
    hd = d // SB_HEADS
    q_out[...] = (_dot(h, w_ref[:, 0:d]) * hd ** -0.5).reshape(bb, tt, d)
    k_out[...] = _dot(h, w_ref[:, d:2 * d]).reshape(bb, tt, d)
    v_out[...] = _dot(h, w_ref[:, 2 * d:3 * d]).reshape(bb, tt, d)
    gate_out[...] = _dot(h, w_ref[:, 3 * d:4 * d]).reshape(bb, tt, d)


def _sb_prompt_kernel(bias_ref, q_ref, k_ref, v_ref, o_ref, acc_scr, z_scr, sp_scr, *, tq):
    p = pl.program_id(1)
    i = pl.program_id(2)
    half = LANES // 2
    q = q_ref[0] * LOG2E
    lane_a = _iota((tq, LANES), 1) < half
    q_st = jnp.concatenate([jnp.where(lane_a, q, 0.0), jnp.where(lane_a, 0.0, q)], axis=0).astype(BF16)
    bias = jnp.where(_iota((2 * tq, 1), 0) < tq, bias_ref[2 * p], bias_ref[2 * p + 1]) * LOG2E
    tri = jnp.where(_iota((tq, tq), 0) >= _iota((tq, tq), 1), 1.0, 0.0).astype(BF16)

    def block_rows(jb):
        return pl.ds(pl.multiple_of(jnp.maximum(jb, 0) * tq, tq), tq)

    def scores(jb):
        z2 = _dot(q_st, k_ref[0, block_rows(jb), :].astype(BF16), _NT) + bias
        neg_abs = pltpu.bitcast(pltpu.bitcast(z2, jnp.uint32) | jnp.uint32(0x80000000), F32)
        sp2 = jnp.maximum(z2, 0.0) + jnp.log2(1.0 + jnp.exp2(neg_abs))
        return z2, sp2

    def score_into(slot, jb):
        z2, sp2 = scores(jb)
        z_scr[slot] = z2
        sp_scr[slot] = sp2.astype(BF16)

    def finish(slot, jb, carry):
        cs = _dot(sp_scr[slot], tri)
        w = jnp.exp2(z_scr[slot] - cs - carry)
        acc_scr[...] += _dot(w.astype(BF16), v_ref[0, block_rows(jb), :].astype(BF16))
        return carry + cs[:, 0:1]

    score_into(0, i - 1)

    z2, sp2 = scores(i)
    before = _iota((2 * tq, tq), 1) < _iota((2 * tq, tq), 0) % tq
    spm = jnp.where(before, sp2, 0.0).astype(BF16)
    cs = _dot(spm, tri)
    w = jnp.where(before, jnp.exp2((z2 - sp2) - (cs - spm.astype(F32))), 0.0)
    acc_scr[...] = _dot(w.astype(BF16), v_ref[0, block_rows(i), :].astype(BF16))
    carry = cs[:, 0:1]

    def pair(n, carry):
        jb = i - 1 - 2 * n
        score_into(1, jb - 1)
        carry = finish(0, jb, carry)
        score_into(0, jb - 2)
        return finish(1, jb - 1, carry)

    carry = lax.fori_loop(0, i // 2, pair, carry)

    @pl.when(i % 2 == 1)
    def _():
        finish(0, 0, carry)

    acc = acc_scr[...]
    o_ref[0] = jnp.where(lane_a, acc[0:tq], acc[tq:2 * tq])


def _sb_prompt(q, k, v, bias):
    B, T, d = q.shape
    tq = min(T, 256)
    qspec = pl.BlockSpec((1, tq, LANES), lambda b, p, i: (b, i, p))
    kspec = pl.BlockSpec((1, T, LANES), lambda b, p, i: (b, 0, p))
    return pl.pallas_call(
        functools.partial(_sb_prompt_kernel, tq=tq),
        grid=(B, d // LANES, T // tq),
        in_specs=[pl.BlockSpec(memory_space=pltpu.SMEM), qspec, kspec, kspec],
        out_specs=qspec,
        out_shape=jax.ShapeDtypeStruct((B, T, d), F32),
        scratch_shapes=[pltpu.VMEM((2 * tq, LANES), F32), pltpu.VMEM((2, 2 * tq, tq), F32),
                        pltpu.VMEM((2, 2 * tq, tq), BF16)],
        compiler_params=_params(("parallel", "parallel", "arbitrary")),
        name="sb_attention_prompt",
    )(bias, q, k, v)


def _sb_paged_kernel(pt_ref, q_ref, kn_ref, vn_ref, bias_ref, kc_ref, vc_ref, o_ref, acc_scr, carry_scr, *, page):
    t, d = q_ref.shape[1], q_ref.shape[2]
    heads = LANES // t
    hd = d // heads
    j = pl.program_id(1)
    q = q_ref[0]
    qrows = jnp.concatenate([q] * heads, axis=0)
    head_of_row = _iota((LANES, d), 0) // t
    qm = jnp.where(_iota((LANES, d), 1) // hd == head_of_row, qrows, 0.0).astype(BF16)
    bias = bias_ref[...]

    def block(kb, vb, before, carry, nk):
        z = _dot(kb.astype(BF16), qm, _NT) + bias
        l = -_softplus(z)
        lm = l if before is None else jnp.where(before, l, 0.0)
        upper = jnp.where(_iota((nk, nk), 1) >= _iota((nk, nk), 0), 1.0, 0.0).astype(BF16)
        cs = _sel_mm(upper, lm)
        w = jnp.exp((z + l) + (cs - lm) + carry)
        if before is not None:
            w = jnp.where(before, w, 0.0)
        return _dot(w.astype(BF16), vb.astype(BF16), _TN), carry + cs[0:1, :]

    @pl.when(j == 0)
    def _():
        nk = 2 * t
        pad = jnp.zeros((nk - t, d), F32)
        before = _iota((nk, LANES), 0) < _iota((nk, LANES), 1) % t
        acc, carry = block(jnp.concatenate([kn_ref[0], pad], axis=0), jnp.concatenate([vn_ref[0], pad], axis=0),
                           before, jnp.zeros((1, LANES), F32), nk)
        acc_scr[...] = acc
        carry_scr[0:1, :] = carry

    acc, carry = block(kc_ref[0], vc_ref[0], None, carry_scr[0:1, :], page)
    acc_scr[...] += acc
    carry_scr[0:1, :] = carry

    @pl.when(j == pl.num_programs(1) - 1)
    def _():
        full = acc_scr[...]
        lane_head = _iota((t, d), 1) // hd
        out = jnp.zeros((t, d), F32)
        for h in range(heads):
            out = jnp.where(lane_head == h, full[h * t:(h + 1) * t, :], out)
        o_ref[0] = out


def _sb_paged(q, k_new, v_new, bias, k_cache, v_cache, page_table):
    B, T, d = q.shape
    n_pool, page = k_cache.shape[0], k_cache.shape[1]
    n_pages = page_table.shape[1]
    heads = SB_HEADS
    assert heads * T == LANES
    tok = pl.BlockSpec((1, T, d), lambda b, j, pt: (b, 0, 0))
    cache = pl.BlockSpec((1, page, d), lambda b, j, pt: (pt[b, n_pages - 1 - j], 0, 0))
    bias_cols = jnp.repeat(bias, T).reshape(1, LANES)
    grid_spec = pltpu.PrefetchScalarGridSpec(
        num_scalar_prefetch=1,
        grid=(B, n_pages),
        in_specs=[tok, tok, tok, pl.BlockSpec((1, LANES), lambda b, j, pt: (0, 0)), cache, cache],
        out_specs=tok,
        scratch_shapes=[pltpu.VMEM((LANES, d), F32), pltpu.VMEM((8, LANES), F32)],
    )
    return pl.pallas_call(
        functools.partial(_sb_paged_kernel, page=page),
        grid_spec=grid_spec,
        out_shape=jax.ShapeDtypeStruct((B, T, d), F32),
        compiler_params=_params(("parallel", "arbitrary")),
        name="sb_attention_paged",
    )(page_table, q, k_new, v_new, bias_cols, k_cache.reshape(n_pool, page, d), v_cache.reshape(n_pool, page, d))


def _sb_post_kernel(o_ref, gate_ref, x_ref, mod_ref, wo_ref, fg_ref, out_ref, *, final):
    bb, tt, d = x_ref.shape
    m = bb * tt
    y = o_ref[...].reshape(m, d) * _silu(gate_ref[...].reshape(m, d))
    _residual_out(y, wo_ref, x_ref, mod_ref, fg_ref, out_ref, final)


def _sb_layer(x, mod, norm_g, final_g, final, past, p):
    B, T, d = x.shape
    consts = [norm_g.reshape(1, d), p['w_in'].astype(BF16)]
    q, k, v, gate = _pre_call(_sb_pre_kernel, "sb_pre", x, mod, consts, (d, d, d, d))
    if past is None:
        o = _sb_prompt(q, k, v, p['bias'])
    else:
        o = _sb_paged(q, k, v, p['bias'], past[0], past[1], past[2])
    x = _post_call(_sb_post_kernel, "sb_post", x, mod, [o, gate], [p['w_o'].astype(BF16), final_g.reshape(1, d)],
                   final)
    hd = d // SB_HEADS
    return x, k.reshape(B, T, SB_HEADS, hd), v.reshape(B, T, SB_HEADS, hd)


def kernel(x_prompt, x_sample, c_prompt, c_sample, state_rwkv, cache_rwkv_shift, state_gla, cache_sb_k, cache_sb_v, state_hgrn, page_table, norm_g, ada_w, ada_b, final_g, rw_mix, rw_w_rkvg, rw_w0, rw_w1, rw_w2, rw_a0, rw_a1, rw_a2, rw_k_k, rw_k_a, rw_r_k, rw_gn_w, rw_gn_b, rw_w_o, gla_w_in, gla_w_gk2, gla_b_gk2, gla_gn_w, gla_w_o, sb_w_in, sb_bias, sb_w_o, hg_w_in, hg_lower, hg_gn_w, hg_w_o):
    depth, d = norm_g.shape
    n_mix = 4
    bp, bs = x_prompt.shape[0], x_sample.shape[0]
    rows = -(-(bp + bs) // 8) * 8
    c_all = jnp.pad(jnp.concatenate([c_prompt, c_sample], axis=0), ((0, rows - bp - bs), (0, 0)))
    mods = _modulation(c_all, ada_w, ada_b)

    def trunk(x, mod_all, rw_s, rw_shift, gla_s, sb_past, hg_s):
        B = x.shape[0]
        outs = {n: [] for n in ('rw_s', 'rw_shift', 'gla_s', 'sb_k', 'sb_v', 'hg_s')}
        for i in range(depth):
            kind, j = i % n_mix, i // n_mix
            mod = mod_all[i].reshape(B, 1, 3 * d)
            final = i == depth - 1
            if kind == 0:
                p = dict(mix=rw_mix[j], w_rkvg=rw_w_rkvg[j], w0=rw_w0[j], w1=rw_w1[j], w2=rw_w2[j], a0=rw_a0[j],
                         a1=rw_a1[j], a2=rw_a2[j], k_k=rw_k_k[j], k_a=rw_k_a[j], r_k=rw_r_k[j], gn_w=rw_gn_w[j],
                         gn_b=rw_gn_b[j], w_o=rw_w_o[j])
                x, s, sh = _rwkv_layer(x, mod, norm_g[i], final_g, final, rw_shift[j], rw_s[j], p)
                outs['rw_s'].append(s)
                outs['rw_shift'].append(sh)
            elif kind == 1:
                p = dict(w_in=gla_w_in[j], w_gk2=gla_w_gk2[j], b_gk2=gla_b_gk2[j], gn_w=gla_gn_w[j], w_o=gla_w_o[j])
                x, s = _gla_layer(x, mod, norm_g[i], final_g, final, gla_s[j], p)
                outs['gla_s'].append(s)
            elif kind == 2:
                p = dict(w_in=sb_w_in[j], bias=sb_bias[j], w_o=sb_w_o[j])
                x, k_new, v_new = _sb_layer(x, mod, norm_g[i], final_g, final, sb_past(j), p)
                outs['sb_k'].append(k_new)
                outs['sb_v'].append(v_new)
            else:
                p = dict(w_in=hg_w_in[j], lower=hg_lower, gn_w=hg_gn_w[j], w_o=hg_w_o[j])
                x, s = _hg_layer(x, mod, norm_g[i], final_g, final, hg_s[j], i, p)
                outs['hg_s'].append(s)
        st = lambda n: jnp.stack(outs[n])
        return x, st('rw_s'), st('rw_shift'), st('gla_s'), st('sb_k'), st('sb_v'), st('hg_s')

    z = lambda a, b: jnp.zeros((a.shape[0], b) + a.shape[2:], a.dtype)
    prompt = trunk(x_prompt, mods[:, :bp], z(state_rwkv, bp), z(cache_rwkv_shift, bp), z(state_gla, bp),
                   lambda j: None, z(state_hgrn, bp))
    sample = trunk(x_sample, mods[:, bp:bp + bs], state_rwkv, cache_rwkv_shift, state_gla,
                   lambda j: (cache_sb_k[j], cache_sb_v[j], page_table), state_hgrn)
    return (prompt[0], sample[0]) + prompt[1:] + sample[1:]
```

```python
import functools
import math

import jax
import jax.numpy as jnp
from jax import lax
from jax.experimental import pallas as pl
from jax.experimental.pallas import tpu as pltpu

F32 = jnp.float32
BF16 = jnp.bfloat16

NORM_EPS = 1e-6
RW_HEAD = 64
RW_GN_EPS = 64e-5
GLA_HEADS = 4
GLA_GATE_NORMALIZER = 16.0
SB_HEADS = 16
HG_HEADS = 8
LANES = 128
ROW_TILE = 512
SUB = 16
GLA_SUB = 16
VMEM_LIMIT = 56 * 1024 * 1024
LOG2E = 1.4426950408889634

_NN = (((1,), (0,)), ((), ()))
_NT = (((1,), (1,)), ((), ()))
_TN = (((0,), (0,)), ((), ()))
_BNN = (((2,), (1,)), ((0,), (0,)))
_BNT = (((2,), (2,)), ((0,), (0,)))
_BTN = (((1,), (1,)), ((0,), (0,)))


def _dot(a, b, dims=_NN):
    return lax.dot_general(a, b, dims, preferred_element_type=F32)


def _split(x):
    hi = x.astype(BF16)
    lo = (x - hi.astype(F32)).astype(BF16)
    return hi, lo


def _mm1(a, b, dims=_NN):
    return _dot(a.astype(BF16), b.astype(BF16), dims)


def _mm3(a, b, dims=_NN):
    ah, al = _split(a)
    bh, bl = _split(b)
    return _dot(ah, bh, dims) + (_dot(ah, bl, dims) + _dot(al, bh, dims))


def _mm_sel(a, sel, dims=_NN):
    ah, al = _split(a)
    return _dot(ah, sel, dims) + _dot(al, sel, dims)


def _mm_sel_fused(a, sel2):
    ah, al = _split(a)
    return _dot(jnp.concatenate([ah, al], axis=1), sel2)


def _sel_mm(sel, b, dims=_NN):
    bh, bl = _split(b)
    return _dot(sel, bh, dims) + _dot(sel, bl, dims)


def _sel_mm3(sel, b):
    b1 = b.astype(BF16)
    r1 = b - b1.astype(F32)
    b2 = r1.astype(BF16)
    b3 = (r1 - b2.astype(F32)).astype(BF16)
    return _dot(sel, b1) + (_dot(sel, b2) + _dot(sel, b3))


def _silu(x):
    return x * (1.0 / (1.0 + jnp.exp(-x)))


def _sigmoid(x):
    return 1.0 / (1.0 + jnp.exp(-x))


def _softplus(x):
    return jnp.maximum(x, 0.0) + jnp.log1p(jnp.exp(-jnp.abs(x)))


def _iota(shape, dim):
    return lax.broadcasted_iota(jnp.int32, shape, dim)


def _head_indicator(d, hs):
    return jnp.where(_iota((d, LANES), 0) // hs == _iota((d, LANES), 1), 1.0, 0.0).astype(BF16)


def _head_indicator_t(d, hs):
    return jnp.where(_iota((LANES, d), 1) // hs == _iota((LANES, d), 0), 1.0, 0.0).astype(BF16)


def _head_sum(x, hs):
    return _mm_sel(x, _head_indicator(x.shape[-1], hs))


def _head_bcast(s, d, hs):
    return _mm_sel(s, _head_indicator_t(d, hs))


def _row_tiles(B, T):
    tt = min(T, ROW_TILE)
    bb = max(1, min(B, ROW_TILE // tt))
    assert T % tt == 0 and B % bb == 0 and tt % 8 == 0
    return bb, tt


def _params(sem):
    return pltpu.CompilerParams(dimension_semantics=sem, vmem_limit_bytes=VMEM_LIMIT)


def _const_spec(shape):
    n = len(shape)
    return pl.BlockSpec(shape, lambda *_: (0,) * n)


def _prenorm(x, g, mod, d):
    shift = mod[:, :, 0:d]
    scale = mod[:, :, d:2 * d]
    y = x * lax.rsqrt(jnp.mean(x * x, axis=-1, keepdims=True) + NORM_EPS)
    return (y * g) * (1.0 + scale) + shift


def _mod_kernel(c_ref, w_ref, b_ref, o_ref):
    o_ref[0] = _mm1(_silu(c_ref[...]), w_ref[0]) + b_ref[0]


def _modulation(c, ada_w, ada_b):
    depth, d, n = ada_w.shape
    rows = c.shape[0]
    tn = 1536
    return pl.pallas_call(
        _mod_kernel,
        grid=(depth, n // tn),
        in_specs=[pl.BlockSpec((rows, d), lambda l, j: (0, 0)),
                  pl.BlockSpec((1, d, tn), lambda l, j: (l, 0, j)),
                  pl.BlockSpec((1, 1, tn), lambda l, j: (l, 0, j))],
        out_specs=pl.BlockSpec((1, rows, tn), lambda l, j: (l, 0, j)),
        out_shape=jax.ShapeDtypeStruct((depth, rows, n), F32),
        compiler_params=_params(("parallel", "parallel")),
        name="ada_modulation",
    )(c, ada_w, ada_b.reshape(depth, 1, n))


def _rwkv_pre_kernel(x_ref, halo_ref, xprev_ref, mod_ref, ng_ref, mix_ref, wr_ref, wk_ref, wv_ref, wg_ref,
                     w0_ref, w1_ref, w2_ref, a0_ref, a1_ref, a2_ref, kk_ref, ka_ref, rk_ref,
                     r_out, ld_out, k_out, v_out, kkn_out, a_out, g_out, bonus_out, shift_out):
    bb, tt, d = x_ref.shape
    m = bb * tt
    mod = mod_ref[...]
    g = ng_ref[...]
    h3 = _prenorm(x_ref[...], g, mod, d)
    h_halo = _prenorm(halo_ref[:, 7:8, :], g, mod, d)
    h_prev = jnp.where(pl.program_id(1) == 0, xprev_ref[...], h_halo)
    shift_out[...] = h3[:, tt - 1:tt, :]
    h = h3.reshape(m, d)
    prev = jnp.where(_iota((m, 1), 0) % tt == 0,
                     jnp.broadcast_to(h_prev, (bb, tt, d)).reshape(m, d),
                     pltpu.roll(h, 1, axis=0))
    xx = prev - h
    xr, xw, xk, xv, xa, xg = (h + xx * mix_ref[n:n + 1, :] for n in range(6))
    r = _mm1(xr, wr_ref[...])
    k = _mm1(xk, wk_ref[...])
    v = _mm1(xv, wv_ref[...])
    gate = _silu(_mm1(xg, wg_ref[...]))
    lw = w0_ref[...] + _mm1(jnp.tanh(_mm1(xw, w1_ref[...])), w2_ref[...])
    log_w = -_softplus(-lw) - 0.5
    a = _sigmoid(a0_ref[...] + _mm1(_mm1(xa, a1_ref[...]), a2_ref[...]))
    kk = k * kk_ref[...]
    nrm = jnp.maximum(jnp.sqrt(_head_sum(kk * kk, RW_HEAD)), 1e-12)
    kk = kk * _head_bcast(1.0 / nrm, d, RW_HEAD)
    k = k * (1.0 + (a - 1.0) * ka_ref[...])
    bonus = _head_bcast(_head_sum(r * k * rk_ref[...], RW_HEAD), d, RW_HEAD) * v
    for ref, val in ((r_out, r), (ld_out, -jnp.exp(log_w)), (k_out, k), (v_out, v), (kkn_out, kk),
                     (a_out, a), (g_out, gate), (bonus_out, bonus)):
        ref[...] = val.reshape(bb, tt, d)


def _pad_cols(w, n):
    return jnp.pad(w, ((0, 0), (0, n - w.shape[1])))


def _pad_rows(w, n):
    return jnp.pad(w, ((0, n - w.shape[0]), (0, 0)))


def _rwkv_pre(x, x_prev, mod, norm_g, p):
    B, T, d = x.shape
    tt = min(T, ROW_TILE // 2)
    bb = max(1, min(B, (ROW_TILE // 2) // tt))
    row = lambda a: a.reshape(1, d)
    tok = pl.BlockSpec((bb, tt, d), lambda i, j: (i, j, 0))
    per_seq = lambda n: pl.BlockSpec((bb, 1, n), lambda i, j: (i, 0, 0))
    halo = pl.BlockSpec((bb, 8, d), lambda i, j: (i, jnp.maximum(j * (tt // 8) - 1, 0), 0))
    consts = [row(norm_g), p['mix'],
              p['w_rkvg'][0].astype(BF16), p['w_rkvg'][1].astype(BF16), p['w_rkvg'][2].astype(BF16),
              p['w_rkvg'][3].astype(BF16),
              row(p['w0']), _pad_cols(p['w1'], LANES).astype(BF16), _pad_rows(p['w2'], LANES).astype(BF16),
              row(p['a0']), _pad_cols(p['a1'], LANES).astype(BF16), _pad_rows(p['a2'], LANES).astype(BF16),
              row(p['k_k']), row(p['k_a']), row(p['r_k'])]
    out = pl.pallas_call(
        _rwkv_pre_kernel,
        grid=(B // bb, T // tt),
        in_specs=[tok, halo, per_seq(d), per_seq(3 * d)] + [_const_spec(c.shape) for c in consts],
        out_specs=[tok] * 8 + [per_seq(d)],
        out_shape=[jax.ShapeDtypeStruct((B, T, d), F32)] * 8 + [jax.ShapeDtypeStruct((B, 1, d), F32)],
        compiler_params=_params(("parallel", "arbitrary")),
        name="rwkv_pre",
    )(x, x, x_prev.reshape(B, 1, d), mod, *consts)
    return out


def _unit_lower_inverse(m_strict, c):
    n = m_strict.shape[-1]
    ti = _iota((n, n), 0)
    si = _iota((n, n), 1)
    eye = jnp.where(ti == si, 1.0, 0.0)
    sub = min(SUB, c)
    d_part = jnp.where(ti // sub == si // sub, m_strict, 0.0)
    t_d = eye + d_part
    pw = d_part
    for _ in range(int(math.log2(sub)) - 1):
        pw = _mm3(pw, pw, _BNN)
        t_d = t_d + _mm3(t_d, pw, _BNN)
    if sub == c:
        return t_d
    assert c // sub == 4
    nn = _mm3(t_d, m_strict - d_part, _BNN)
    n2 = _mm3(nn, nn, _BNN)
    return _mm3(eye + nn + n2 + _mm3(nn, n2, _BNN), t_d, _BNN)


def _rwkv_chunk_kernel(r_ref, ld_ref, k_ref, v_ref, kk_ref, a_ref, s0_ref, o_ref, s_out_ref, s_scr, *, c):
    ns, tb, d = r_ref.shape
    n = ns * c
    pairs = d // LANES
    half = LANES // 2
    j = pl.program_id(1)

    @pl.when(j == 0)
    def _():
        z = jnp.zeros((half, half), F32)
        for s in range(ns):
            for p in range(pairs):
                top = jnp.concatenate([s0_ref[s, 2 * p], z], axis=1)
                bot = jnp.concatenate([z, s0_ref[s, 2 * p + 1]], axis=1)
                s_scr[s, p] = jnp.concatenate([top, bot], axis=0)

    lane_a = _iota((n, LANES), 1) < half
    ti = _iota((2 * n, 2 * n), 0)
    si = _iota((2 * n, 2 * n), 1)
    same = ti // c == si // c
    strict = same & (si < ti)
    incl = same & (si <= ti)
    tc = _iota((n, n), 0)
    sc = _iota((n, n), 1)
    tri = jnp.where((tc // c == sc // c) & (sc <= tc), 1.0, 0.0).astype(BF16)
    last = jnp.where((tc // c == sc // c) & (sc % c == c - 1), 1.0, 0.0).astype(BF16)
    bd = (_iota((LANES, LANES), 0) < half) == (_iota((LANES, LANES), 1) < half)

    def load(ref):
        x = ref[...]
        if tb < c:
            x = jnp.concatenate([x, jnp.zeros((ns, c - tb, d), F32)], axis=1)
        return x.reshape(n, d)

    def by_pair(x):
        return jnp.stack([x[:, p * LANES:(p + 1) * LANES] for p in range(pairs)], axis=0)

    def stack(x):
        return jnp.concatenate([jnp.where(lane_a, x, 0.0), jnp.where(lane_a, 0.0, x)], axis=1)

    def unstack(x):
        return x[:, 0:n] + x[:, n:2 * n]

    r, ld, k, v, kk, a = (load(ref) for ref in (r_ref, ld_ref, k_ref, v_ref, kk_ref, a_ref))
    cum = _sel_mm3(tri, ld)
    cum_last = _sel_mm3(last, cum)
    g_end = jnp.exp(cum_last - cum)
    g_inv = jnp.exp(-cum)
    beta = kk * a
    a_t = by_pair(-kk * jnp.exp(cum - ld))
    r_t = by_pair(r * jnp.exp(cum))
    k_t = by_pair(k * g_inv)
    b_t = by_pair(beta * g_inv)
    at_st = stack(a_t)
    rt_st = stack(r_t)
    k2 = jnp.concatenate([k_t, k_t], axis=1)
    b2 = jnp.concatenate([b_t, b_t], axis=1)
    m_k = jnp.where(strict, _mm1(at_st, k2, _BNT), 0.0)
    m_b = jnp.where(strict, _mm1(at_st, b2, _BNT), 0.0)
    a_k = jnp.where(incl, _mm1(rt_st, k2, _BNT), 0.0)
    a_b = jnp.where(incl, _mm1(rt_st, b2, _BNT), 0.0)
    t_inv = _unit_lower_inverse(m_b, c)
    v_p = by_pair(v)
    v_st = stack(v_p)
    states = [s_scr[s] for s in range(ns)]
    w1 = jnp.concatenate([_mm1(a_t[:, s * c:(s + 1) * c], states[s], _BNT) for s in range(ns)], axis=1)
    o1 = jnp.concatenate([_mm1(r_t[:, s * c:(s + 1) * c], states[s], _BNT) for s in range(ns)], axis=1)
    sa_st = _mm1(t_inv, stack(w1) + _mm1(m_k, v_st, _BNN), _BNN)
    o = o1 + unstack(_mm1(a_k, v_st, _BNN) + _mm1(a_b, sa_st, _BNN))
    sa = unstack(sa_st)
    for p in range(pairs):
        o_ref[:, :, p * LANES:(p + 1) * LANES] = o[p].reshape(ns, c, LANES)[:, 0:tb, :]
    kg = by_pair(k * g_end)
    bg = by_pair(beta * g_end)
    for s in range(ns):
        rows = slice(s * c, (s + 1) * c)
        upd = _mm1(v_p[:, rows], kg[:, rows], _BTN) + _mm1(sa[:, rows], bg[:, rows], _BTN)
        decay = by_pair(jnp.exp(cum_last[s * c:s * c + 1, :]))
        s_scr[s] = states[s] * decay + jnp.where(bd, upd, 0.0)

    @pl.when(j == pl.num_programs(1) - 1)
    def _():
        for s in range(ns):
            for p in range(pairs):
                blk = s_scr[s, p]
                s_out_ref[s, 2 * p] = blk[0:half, 0:half]
                s_out_ref[s, 2 * p + 1] = blk[half:LANES, half:LANES]


def _rwkv_chunk(r, ld, k, v, kk, a, s0):
    B, T, d = r.shape
    heads = d // RW_HEAD
    if T >= 64:
        c, ns, tb = 64, 1, 64
    else:
        c, ns, tb = 16, 4, T
    assert T % tb == 0 and B % ns == 0 and tb <= c
    tok = pl.BlockSpec((ns, tb, d), lambda i, j: (i, j, 0))
    st = pl.BlockSpec((ns, heads, RW_HEAD, RW_HEAD), lambda i, j: (i, 0, 0, 0))
    return pl.pallas_call(
        functools.partial(_rwkv_chunk_kernel, c=c),
        grid=(B // ns, T // tb),
        in_specs=[tok] * 6 + [st],
        out_specs=[tok, st],
        out_shape=[jax.ShapeDtypeStruct((B, T, d), F32),
                   jax.ShapeDtypeStruct((B, heads, RW_HEAD, RW_HEAD), F32)],
        scratch_shapes=[pltpu.VMEM((ns, d // LANES, LANES, LANES), F32)],
        compiler_params=_params(("parallel", "arbitrary")),
        name="rwkv_chunk",
    )(r, ld, k, v, kk, a, s0)


def _residual_out(y, w_ref, x_ref, mod_ref, fg_ref, o_ref, final):
    bb, tt, d = x_ref.shape
    out = _mm1(y, w_ref[...]).reshape(bb, tt, d)
    x = x_ref[...] + mod_ref[:, :, 2 * d:3 * d] * out
    if final:
        x = x * lax.rsqrt(jnp.mean(x * x, axis=-1, keepdims=True) + NORM_EPS) * fg_ref[...]
    o_ref[...] = x


def _rwkv_post_kernel(o_ref, bonus_ref, g_ref, x_ref, mod_ref, gnw_ref, gnb_ref, wo_ref, fg_ref, out_ref, *, final):
    bb, tt, d = x_ref.shape
    m = bb * tt
    o = o_ref[...].reshape(m, d)
    mu = _head_bcast(_head_sum(o, RW_HEAD) * (1.0 / RW_HEAD), d, RW_HEAD)
    oc = o - mu
    var = _head_bcast(_head_sum(oc * oc, RW_HEAD) * (1.0 / RW_HEAD), d, RW_HEAD)
    y = oc * lax.rsqrt(var + RW_GN_EPS) * gnw_ref[...] + gnb_ref[...]
    y = (y + bonus_ref[...].reshape(m, d)) * g_ref[...].reshape(m, d)
    _residual_out(y, wo_ref, x_ref, mod_ref, fg_ref, out_ref, final)


def _post_call(kernel, name, x, mod, tok_inputs, consts, final):
    B, T, d = x.shape
    bb, tt = _row_tiles(B, T)
    tok = lambda a: pl.BlockSpec((bb, tt, a.shape[-1]), lambda i, j: (i, j, 0))
    return pl.pallas_call(
        functools.partial(kernel, final=final),
        grid=(B // bb, T // tt),
        in_specs=[tok(a) for a in tok_inputs] + [tok(x), pl.BlockSpec((bb, 1, 3 * d), lambda i, j: (i, 0, 0))]
        + [_const_spec(c.shape) for c in consts],
        out_specs=tok(x),
        out_shape=jax.ShapeDtypeStruct((B, T, d), F32),
        compiler_params=_params(("parallel", "parallel")),
        name=name,
    )(*tok_inputs, x, mod, *consts)


def _rwkv_layer(x, mod, norm_g, final_g, final, x_prev, s0, p):
    d = x.shape[-1]
    r, ld, k, v, kk, a, g, bonus, shift = _rwkv_pre(x, x_prev, mod, norm_g, p)
    o, s_new = _rwkv_chunk(r, ld, k, v, kk, a, s0)
    consts = [p['gn_w'].reshape(1, d), p['gn_b'].reshape(1, d), p['w_o'].astype(BF16), final_g.reshape(1, d)]
    x = _post_call(_rwkv_post_kernel, "rwkv_post", x, mod, [o, bonus, g], consts, final)
    return x, s_new, shift[:, 0, :]


def _gla_pre_kernel(x_ref, mod_ref, ng_ref, w_ref, wl_ref, w2_ref, b2_ref, q_out, k_out, v_out, gate_out, gk_out,
                    *, qk):
    bb, tt, d = x_ref.shape
    m = bb * tt
    h = _prenorm(x_ref[...], ng_ref[...], mod_ref[...], d).reshape(m, d).astype(BF16)
    dk = qk // GLA_HEADS
    q_out[...] = (_dot(h, w_ref[:, 0:qk]) * dk ** -0.5).reshape(bb, tt, qk)
    k_out[...] = _dot(h, w_ref[:, qk:2 * qk]).reshape(bb, tt, qk)
    v_out[...] = _dot(h, w_ref[:, 2 * qk:2 * qk + d]).reshape(bb, tt, d)
    gate_out[...] = _dot(h, w_ref[:, 2 * qk + d:2 * qk + 2 * d]).reshape(bb, tt, d)
    low = _dot(h, wl_ref[...])
    gk = _mm3(low, w2_ref[...]) + b2_ref[...]
    gk_out[...] = (-_softplus(-gk) * (1.0 / GLA_GATE_NORMALIZER)).reshape(bb, tt, qk)


def _hg_pre_kernel(x_ref, mod_ref, ng_ref, w_ref, lower_ref, q_out, k_out, v_out, gate_out, g_out, *, layer):
    bb, tt, d = x_ref.shape
    m = bb * tt
    h = _prenorm(x_ref[...], ng_ref[...], mod_ref[...], d).reshape(m, d).astype(BF16)
    dk = d // HG_HEADS
    low = lower_ref[...]
    e = jnp.exp(low - jnp.max(low, axis=0, keepdims=True))
    soft = e / jnp.sum(e, axis=0, keepdims=True)
    lb = jnp.sum(soft[0:layer + 1], axis=0, keepdims=True) - soft[0:1]
    q = _dot(h, w_ref[:, 0:d])
    f = _dot(h, w_ref[:, d:2 * d])
    forget = lb + (1.0 - lb) * _sigmoid(f)
    q_out[...] = (_silu(q) * dk ** -0.5).reshape(bb, tt, d)
    k_out[...] = (1.0 - forget).reshape(bb, tt, d)
    g_out[...] = jnp.log(forget).reshape(bb, tt, d)
    v_out[...] = _dot(h, w_ref[:, 2 * d:3 * d]).reshape(bb, tt, d)
    gate_out[...] = _dot(h, w_ref[:, 3 * d:4 * d]).reshape(bb, tt, d)


def _pre_call(kernel, name, x, mod, consts, out_widths):
    B, T, d = x.shape
    bb, tt = _row_tiles(B, T)
    tok = lambda n: pl.BlockSpec((bb, tt, n), lambda i, j: (i, j, 0))
    return pl.pallas_call(
        kernel,
        grid=(B // bb, T // tt),
        in_specs=[tok(d), pl.BlockSpec((bb, 1, 3 * d), lambda i, j: (i, 0, 0))]
        + [_const_spec(c.shape) for c in consts],
        out_specs=[tok(n) for n in out_widths],
        out_shape=[jax.ShapeDtypeStruct((B, T, n), F32) for n in out_widths],
        compiler_params=_params(("parallel", "parallel")),
        name=name,
    )(x, mod, *consts)


def _gla_chunk_kernel(q_ref, k_ref, g_ref, v_ref, s0_ref, o_ref, s_out_ref, st_scr, *, c, heads):
    tb = q_ref.shape[1]
    dk = q_ref.shape[2] // heads
    dv = v_ref.shape[2] // heads
    j = pl.program_id(1)

    @pl.when(j == 0)
    def _():
        for h in range(heads):
            st_scr[h] = s0_ref[0, h].T

    def load(ref, w):
        x = ref[0]
        if tb < c:
            x = jnp.concatenate([x, jnp.zeros((c - tb, x.shape[1]), F32)], axis=0)
        return jnp.stack([x[:, h * w:(h + 1) * w] for h in range(heads)], axis=0)

    q, k, g, v = load(q_ref, dk), load(k_ref, dk), load(g_ref, dk), load(v_ref, dv)
    tc = _iota((c, c), 0)
    sc = _iota((c, c), 1)
    tri = jnp.where(sc <= tc, 1.0, 0.0).astype(BF16)
    b = jnp.stack([_sel_mm3(tri, g[h]) for h in range(heads)], axis=0)
    st = st_scr[...]
    o_inter = _mm3(q * jnp.exp(b), st, _BNT)
    sub = min(GLA_SUB, c)
    key_row = _iota((c, 1), 0)
    row = _iota((sub, 1), 0)
    outs = []
    for i in range(c // sub):
        r0 = i * sub
        qi = q[:, r0:r0 + sub]
        bi = b[:, r0:r0 + sub]
        acc = o_inter[:, r0:r0 + sub]
        if i > 0:
            ref_b = b[:, r0 - 1:r0]
            k_fac = jnp.exp(jnp.where(key_row < r0, ref_b - b, -jnp.inf))
            att = _mm3(qi * jnp.exp(bi - ref_b), k * k_fac, _BNT)
            acc = acc + _mm3(att, v, _BNN)
        for s in range(sub):
            diff = jnp.where(row >= s, bi - bi[:, s:s + 1], -jnp.inf)
            col = jnp.sum(qi * k[:, r0 + s:r0 + s + 1] * jnp.exp(diff), axis=2, keepdims=True)
            acc = acc + col * v[:, r0 + s:r0 + s + 1]
        outs.append(acc)
    o = jnp.concatenate(outs, axis=1) if len(outs) > 1 else outs[0]
    for h in range(heads):
        o_ref[0, :, h * dv:(h + 1) * dv] = o[h, 0:tb]
    b_last = b[:, c - 1:c]
    st_new = st * jnp.exp(b_last) + _mm3(v, k * jnp.exp(b_last - b), _BTN)
    st_scr[...] = st_new

    @pl.when(j == pl.num_programs(1) - 1)
    def _():
        for h in range(heads):
            s_out_ref[0, h] = st_new[h].T


def _gla_chunk(q, k, g, v, s0, heads):
    B, T, qk = q.shape
    d = v.shape[-1]
    dk = qk // heads
    dv = d // heads
    tb = 64 if T % 64 == 0 else T
    c = -(-tb // SUB) * SUB
    kspec = pl.BlockSpec((1, tb, qk), lambda b, j: (b, j, 0))
    vspec = pl.BlockSpec((1, tb, d), lambda b, j: (b, j, 0))
    sspec = pl.BlockSpec((1, heads, dk, dv), lambda b, j: (b, 0, 0, 0))
    return pl.pallas_call(
        functools.partial(_gla_chunk_kernel, c=c, heads=heads),
        grid=(B, T // tb),
        in_specs=[kspec, kspec, kspec, vspec, sspec],
        out_specs=[vspec, sspec],
        out_shape=[jax.ShapeDtypeStruct((B, T, d), F32),
                   jax.ShapeDtypeStruct((B, heads, dk, dv), F32)],
        scratch_shapes=[pltpu.VMEM((heads, dv, dk), F32)],
        compiler_params=_params(("parallel", "arbitrary")),
        name="gla_chunk",
    )(q, k, g, v, s0)


def _gla_post_kernel(o_ref, gate_ref, x_ref, mod_ref, gnw_ref, wo_ref, fg_ref, out_ref, *, final, heads):
    bb, tt, d = x_ref.shape
    m = bb * tt
    hs = d // heads
    o = o_ref[...].reshape(m, d)
    ms = _head_bcast(_head_sum(o * o, hs) * (1.0 / hs), d, hs)
    y = o * lax.rsqrt(ms + NORM_EPS) * gnw_ref[...]
    y = y * _silu(gate_ref[...].reshape(m, d))
    _residual_out(y, wo_ref, x_ref, mod_ref, fg_ref, out_ref, final)


def _gla_layer(x, mod, norm_g, final_g, final, s0, p):
    d = x.shape[-1]
    qk = p['w_gk2'].shape[1]
    w_in = p['w_in']
    consts = [norm_g.reshape(1, d), w_in[:, :2 * qk + 2 * d].astype(BF16),
              _pad_cols(w_in[:, 2 * qk + 2 * d:], LANES).astype(BF16), _pad_rows(p['w_gk2'], LANES),
              p['b_gk2'].reshape(1, qk)]
    q, k, v, gate, gk = _pre_call(functools.partial(_gla_pre_kernel, qk=qk), "gla_pre", x, mod, consts,
                                  (qk, qk, d, d, qk))
    o, s_new = _gla_chunk(q, k, gk, v, s0, GLA_HEADS)
    post_consts = [jnp.tile(p['gn_w'], GLA_HEADS).reshape(1, d), p['w_o'].astype(BF16), final_g.reshape(1, d)]
    x = _post_call(functools.partial(_gla_post_kernel, heads=GLA_HEADS), "gla_post", x, mod, [o, gate], post_consts,
                   final)
    return x, s_new


def _hg_layer(x, mod, norm_g, final_g, final, s0, layer, p):
    d = x.shape[-1]
    consts = [norm_g.reshape(1, d), p['w_in'].astype(BF16), p['lower']]
    q, k, v, gate, g = _pre_call(functools.partial(_hg_pre_kernel, layer=layer), "hgrn_pre", x, mod, consts,
                                 (d, d, d, d, d))
    o, s_new = _gla_chunk(q, k, g, v, s0, HG_HEADS)
    post_consts = [jnp.tile(p['gn_w'], HG_HEADS).reshape(1, d), p['w_o'].astype(BF16), final_g.reshape(1, d)]
    x = _post_call(functools.partial(_gla_post_kernel, heads=HG_HEADS), "hgrn_post", x, mod, [o, gate], post_consts,
                   final)
    return x, s_new


def _sb_pre_kernel(x_ref, mod_ref, ng_ref, w_ref, q_out, k_out, v_out, gate_out):
    bb, tt, d = x_ref.shape
    m = bb * tt
    h = _prenorm(x_ref[...], ng_ref[...], mod_ref[...], d).reshape(m, d).astype(BF16)
    hd = d // SB_HEADS
    q_out[...] = (_dot(h, w_ref[:, 0:d]) * hd ** -0.5).reshape(bb, tt, d)
    k_out[...] = _dot(h, w_ref[:, d:2 * d]).reshape(bb, tt, d)
    v_out[...] = _dot(h, w_ref[:, 2 * d:3 * d]).reshape(bb, tt, d)
    gate_out[...] = _dot(h, w_ref[:, 3 * d:4 * d]).reshape(bb, tt, d)


def _sb_prompt_kernel(bias_ref, q_ref, k_ref, v_ref, o_ref, acc_scr, z_scr, sp_scr, *, tq):
    p = pl.program_id(1)
    i = pl.program_id(2)
    half = LANES // 2
    q = q_ref[0] * LOG2E
    lane_a = _iota((tq, LANES), 1) < half
    q_st = jnp.concatenate([jnp.where(lane_a, q, 0.0), jnp.where(lane_a, 0.0, q)], axis=0).astype(BF16)
    bias = jnp.where(_iota((2 * tq, 1), 0) < tq, bias_ref[2 * p], bias_ref[2 * p + 1]) * LOG2E
    tri = jnp.where(_iota((tq, tq), 0) >= _iota((tq, tq), 1), 1.0, 0.0).astype(BF16)

    def block_rows(jb):
        return pl.ds(pl.multiple_of(jnp.maximum(jb, 0) * tq, tq), tq)

    def scores(jb):
        z2 = _dot(q_st, k_ref[0, block_rows(jb), :].astype(BF16), _NT) + bias
        sp2 = jnp.maximum(z2, 0.0) + jnp.log2(1.0 + jnp.exp2(-jnp.abs(z2)))
        return z2, sp2

    def score_into(slot, jb):
        z2, sp2 = scores(jb)
        z_scr[slot] = z2
        sp_scr[slot] = sp2.astype(BF16)

    def finish(slot, jb, carry):
        cs = _dot(sp_scr[slot], tri)
        w = jnp.exp2(z_scr[slot] - cs - carry)
        acc_scr[...] += _dot(w.astype(BF16), v_ref[0, block_rows(jb), :].astype(BF16))
        return carry + cs[:, 0:1]

    score_into(0, i - 1)

    z2, sp2 = scores(i)
    before = _iota((2 * tq, tq), 1) < _iota((2 * tq, tq), 0) % tq
    spm = jnp.where(before, sp2, 0.0).astype(BF16)
    cs = _dot(spm, tri)
    w = jnp.where(before, jnp.exp2((z2 - sp2) - (cs - spm.astype(F32))), 0.0)
    acc_scr[...] = _dot(w.astype(BF16), v_ref[0, block_rows(i), :].astype(BF16))
    carry = cs[:, 0:1]

    def pair(n, carry):
        jb = i - 1 - 2 * n
        score_into(1, jb - 1)
        carry = finish(0, jb, carry)
        score_into(0, jb - 2)
        return finish(1, jb - 1, carry)

    carry = lax.fori_loop(0, i // 2, pair, carry)

    @pl.when(i % 2 == 1)
    def _():
        finish(0, 0, carry)

    acc = acc_scr[...]
    o_ref[0] = jnp.where(lane_a, acc[0:tq], acc[tq:2 * tq])


def _sb_prompt(q, k, v, bias):
    B, T, d = q.shape
    tq = min(T, 256)
    qspec = pl.BlockSpec((1, tq, LANES), lambda b, p, i: (b, i, p))
    kspec = pl.BlockSpec((1, T, LANES), lambda b, p, i: (b, 0, p))
    return pl.pallas_call(
        functools.partial(_sb_prompt_kernel, tq=tq),
        grid=(B, d // LANES, T // tq),
        in_specs=[pl.BlockSpec(memory_space=pltpu.SMEM), qspec, kspec, kspec],
        out_specs=qspec,
        out_shape=jax.ShapeDtypeStruct((B, T, d), F32),
        scratch_shapes=[pltpu.VMEM((2 * tq, LANES), F32), pltpu.VMEM((2, 2 * tq, tq), F32),
                        pltpu.VMEM((2, 2 * tq, tq), BF16)],
        compiler_params=_params(("parallel", "parallel", "arbitrary")),
        name="sb_attention_prompt",
    )(bias, q, k, v)


def _sb_paged_kernel(pt_ref, q_ref, kn_ref, vn_ref, bias_ref, *refs, n_blocks):
    kc_refs, vc_refs = refs[:n_blocks], refs[n_blocks:2 * n_blocks]
    o_ref, acc_scr, carry_scr = refs[2 * n_blocks:]
    t, d = q_ref.shape[1], q_ref.shape[2]
    heads = LANES // t
    hd = d // heads
    j = pl.program_id(1)
    q = q_ref[0] * LOG2E
    q_h = [q[:, h * hd:(h + 1) * hd].astype(BF16) for h in range(heads)]
    bias = bias_ref[...] * LOG2E

    def attend(k_of_head, v_of_head, nk, before, carry):
        z2 = jnp.concatenate([_dot(q_h[h], k_of_head(h), _NT) for h in range(heads)], axis=0) + bias[:, 0:nk]
        sp2 = jnp.maximum(z2, 0.0) + jnp.log2(1.0 + jnp.exp2(-jnp.abs(z2)))
        spb = (sp2 if before is None else jnp.where(before, sp2, 0.0)).astype(BF16)
        tri = jnp.where(_iota((nk, nk), 0) >= _iota((nk, nk), 1), 1.0, 0.0).astype(BF16)
        cs = _dot(spb, tri)
        if before is None:
            w = jnp.exp2(z2 - cs - carry)
        else:
            w = jnp.where(before, jnp.exp2((z2 - sp2) - (cs - spb.astype(F32))), 0.0)
        wb = w.astype(BF16)
        upd = jnp.concatenate([_dot(wb[h * t:(h + 1) * t], v_of_head(h)) for h in range(heads)], axis=0)
        total = _dot(spb, jnp.ones((nk, LANES), BF16))
        return upd, total

    @pl.when(j == 0)
    def _():
        nk = 2 * t
        pad = jnp.zeros((nk - t, hd), F32)
        kn, vn = kn_ref[0], vn_ref[0]
        before = _iota((LANES, nk), 1) < _iota((LANES, nk), 0) % t
        upd, total = attend(lambda h: jnp.concatenate([kn[:, h * hd:(h + 1) * hd], pad], axis=0).astype(BF16),
                            lambda h: jnp.concatenate([vn[:, h * hd:(h + 1) * hd], pad], axis=0).astype(BF16),
                            nk, before, None)
        acc_scr[...] = upd
        carry_scr[...] = total

    carry = carry_scr[...]
    acc = acc_scr[...]
    for kc_ref, vc_ref in zip(kc_refs, vc_refs):
        page = kc_ref.shape[1] // heads
        upd, total = attend(lambda h: kc_ref[0, pl.ds(h, page, stride=heads), :].astype(BF16),
                            lambda h: vc_ref[0, pl.ds(h, page, stride=heads), :].astype(BF16), page, None, carry)
        acc = acc + upd
        carry = carry + total
    acc_scr[...] = acc
    carry_scr[...] = carry

    @pl.when(j == pl.num_programs(1) - 1)
    def _():
        o_ref[0] = jnp.concatenate([acc[h * t:(h + 1) * t, :] for h in range(heads)], axis=1)


def _sb_paged(q, k_new, v_new, bias, k_cache, v_cache, page_table):
    B, T, d = q.shape
    n_pool, page, heads, hd = k_cache.shape
    n_pages = page_table.shape[1]
    n_blocks = 4 if n_pages % 4 == 0 else 1
    assert heads * T == LANES and heads * hd == d
    tok = pl.BlockSpec((1, T, d), lambda b, j, pt: (b, 0, 0))

    def cache(i):
        return pl.BlockSpec((1, page * heads, hd), lambda b, j, pt: (pt[b, n_pages - 1 - (j * n_blocks + i)], 0, 0))

    bias_rows = jnp.broadcast_to(jnp.repeat(bias, T)[:, None], (LANES, LANES))
    grid_spec = pltpu.PrefetchScalarGridSpec(
        num_scalar_prefetch=1,
        grid=(B, n_pages // n_blocks),
        in_specs=[tok, tok, tok, pl.BlockSpec((LANES, LANES), lambda b, j, pt: (0, 0))]
        + [cache(i) for i in range(n_blocks)] * 2,
        out_specs=tok,
        scratch_shapes=[pltpu.VMEM((LANES, hd), F32), pltpu.VMEM((LANES, LANES), F32)],
    )
    return pl.pallas_call(
        functools.partial(_sb_paged_kernel, n_blocks=n_blocks),
        grid_spec=grid_spec,
        out_shape=jax.ShapeDtypeStruct((B, T, d), F32),
        compiler_params=_params(("parallel", "arbitrary")),
        name="sb_attention_paged",
    )(page_table, q, k_new, v_new, bias_rows, *([k_cache.reshape(n_pool, page * heads, hd)] * n_blocks),
      *([v_cache.reshape(n_pool, page * heads, hd)] * n_blocks))


def _sb_post_kernel(o_ref, gate_ref, x_ref, mod_ref, wo_ref, fg_ref, out_ref, *, final):
    bb, tt, d = x_ref.shape
    m = bb * tt
    y = o_ref[...].reshape(m, d) * _silu(gate_ref[...].reshape(m, d))
    _residual_out(y, wo_ref, x_ref, mod_ref, fg_ref, out_ref, final)


def _sb_layer(x, mod, norm_g, final_g, final, past, p):
    B, T, d = x.shape
    consts = [norm_g.reshape(1, d), p['w_in'].astype(BF16)]
    q, k, v, gate = _pre_call(_sb_pre_kernel, "sb_pre", x, mod, consts, (d, d, d, d))
    if past is None:
        o = _sb_prompt(q, k, v, p['bias'])
    else:
        o = _sb_paged(q, k, v, p['bias'], past[0], past[1], past[2])
    x = _post_call(_sb_post_kernel, "sb_post", x, mod, [o, gate], [p['w_o'].astype(BF16), final_g.reshape(1, d)],
                   final)
    hd = d // SB_HEADS
    return x, k.reshape(B, T, SB_HEADS, hd), v.reshape(B, T, SB_HEADS, hd)


def kernel(x_prompt, x_sample, c_prompt, c_sample, state_rwkv, cache_rwkv_shift, state_gla, cache_sb_k, cache_sb_v, state_hgrn, page_table, norm_g, ada_w, ada_b, final_g, rw_mix, rw_w_rkvg, rw_w0, rw_w1, rw_w2, rw_a0, rw_a1, rw_a2, rw_k_k, rw_k_a, rw_r_k, rw_gn_w, rw_gn_b, rw_w_o, gla_w_in, gla_w_gk2, gla_b_gk2, gla_gn_w, gla_w_o, sb_w_in, sb_bias, sb_w_o, hg_w_in, hg_lower, hg_gn_w, hg_w_o):
    depth, d = norm_g.shape
    n_mix = 4
    bp, bs = x_prompt.shape[0], x_sample.shape[0]
    rows = -(-(bp + bs) // 8) * 8
    c_all = jnp.pad(jnp.concatenate([c_prompt, c_sample], axis=0), ((0, rows - bp - bs), (0, 0)))
    mods = _modulation(c_all, ada_w, ada_b)

    def trunk(x, mod_all, rw_s, rw_shift, gla_s, sb_past, hg_s):
        B = x.shape[0]
        outs = {n: [] for n in ('rw_s', 'rw_shift', 'gla_s', 'sb_k', 'sb_v', 'hg_s')}
        for i in range(depth):
            kind, j = i % n_mix, i // n_mix
            mod = mod_all[i].reshape(B, 1, 3 * d)
            final = i == depth - 1
            if kind == 0:
                p = dict(mix=rw_mix[j], w_rkvg=rw_w_rkvg[j], w0=rw_w0[j], w1=rw_w1[j], w2=rw_w2[j], a0=rw_a0[j],
                         a1=rw_a1[j], a2=rw_a2[j], k_k=rw_k_k[j], k_a=rw_k_a[j], r_k=rw_r_k[j], gn_w=rw_gn_w[j],
                         gn_b=rw_gn_b[j], w_o=rw_w_o[j])
                x, s, sh = _rwkv_layer(x, mod, norm_g[i], final_g, final, rw_shift[j], rw_s[j], p)
                outs['rw_s'].append(s)
                outs['rw_shift'].append(sh)
            elif kind == 1:
                p = dict(w_in=gla_w_in[j], w_gk2=gla_w_gk2[j], b_gk2=gla_b_gk2[j], gn_w=gla_gn_w[j], w_o=gla_w_o[j])
                x, s = _gla_layer(x, mod, norm_g[i], final_g, final, gla_s[j], p)
                outs['gla_s'].append(s)
            elif kind == 2:
                p = dict(w_in=sb_w_in[j], bias=sb_bias[j], w_o=sb_w_o[j])
                x, k_new, v_new = _sb_layer(x, mod, norm_g[i], final_g, final, sb_past(j), p)
                outs['sb_k'].append(k_new)
                outs['sb_v'].append(v_new)
            else:
                p = dict(w_in=hg_w_in[j], lower=hg_lower, gn_w=hg_gn_w[j], w_o=hg_w_o[j])
                x, s = _hg_layer(x, mod, norm_g[i], final_g, final, hg_s[j], i, p)
                outs['hg_s'].append(s)
        st = lambda n: jnp.stack(outs[n])
        return x, st('rw_s'), st('rw_shift'), st('gla_s'), st('sb_k'), st('sb_v'), st('hg_s')

    z = lambda a, b: jnp.zeros((a.shape[0], b) + a.shape[2:], a.dtype)
    prompt = trunk(x_prompt, mods[:, :bp], z(state_rwkv, bp), z(cache_rwkv_shift, bp), z(state_gla, bp),
                   lambda j: None, z(state_hgrn, bp))
    sample = trunk(x_sample, mods[:, bp:bp + bs], state_rwkv, cache_rwkv_shift, state_gla,
                   lambda j: (cache_sb_k[j], cache_sb_v[j], page_table), state_hgrn)
    return (prompt[0], sample[0]) + prompt[1:] + sample[1:]
```

```python
import functools
import math

import jax
import jax.numpy as jnp
from jax import lax
from jax.experimental import pallas as pl
from jax.experimental.pallas import tpu as pltpu

F32 = jnp.float32
BF16 = jnp.bfloat16

NORM_EPS = 1e-6
RW_HEAD = 64
RW_GN_EPS = 64e-5
GLA_HEADS = 4
GLA_GATE_NORMALIZER = 16.0
SB_HEADS = 16
HG_HEADS = 8
LANES = 128
ROW_TILE = 512
SUB = 16
GLA_SUB = 16
VMEM_LIMIT = 56 * 1024 * 1024
LOG2E = 1.4426950408889634

_NN = (((1,), (0,)), ((), ()))
_NT = (((1,), (1,)), ((), ()))
_TN = (((0,), (0,)), ((), ()))
_BNN = (((2,), (1,)), ((0,), (0,)))
_BNT = (((2,), (2,)), ((0,), (0,)))
_BTN = (((1,), (1,)), ((0,), (0,)))


def _dot(a, b, dims=_NN):
    return lax.dot_general(a, b, dims, preferred_element_type=F32)


def _split(x):
    hi = x.astype(BF16)
    lo = (x - hi.astype(F32)).astype(BF16)
    return hi, lo


def _mm1(a, b, dims=_NN):
    return _dot(a.astype(BF16), b.astype(BF16), dims)


def _mm3(a, b, dims=_NN):
    ah, al = _split(a)
    bh, bl = _split(b)
    return _dot(ah, bh, dims) + (_dot(ah, bl, dims) + _dot(al, bh, dims))


def _mm_sel(a, sel, dims=_NN):
    ah, al = _split(a)
    return _dot(ah, sel, dims) + _dot(al, sel, dims)


def _mm_sel_fused(a, sel2):
    ah, al = _split(a)
    return _dot(jnp.concatenate([ah, al], axis=1), sel2)


def _sel_mm(sel, b, dims=_NN):
    bh, bl = _split(b)
    return _dot(sel, bh, dims) + _dot(sel, bl, dims)


def _sel_mm3(sel, b):
    b1 = b.astype(BF16)
    r1 = b - b1.astype(F32)
    b2 = r1.astype(BF16)
    b3 = (r1 - b2.astype(F32)).astype(BF16)
    return _dot(sel, b1) + (_dot(sel, b2) + _dot(sel, b3))


def _silu(x):
    return x * (1.0 / (1.0 + jnp.exp(-x)))


def _sigmoid(x):
    return 1.0 / (1.0 + jnp.exp(-x))


def _softplus(x):
    return jnp.maximum(x, 0.0) + jnp.log1p(jnp.exp(-jnp.abs(x)))


def _iota(shape, dim):
    return lax.broadcasted_iota(jnp.int32, shape, dim)


def _head_indicator(d, hs):
    return jnp.where(_iota((d, LANES), 0) // hs == _iota((d, LANES), 1), 1.0, 0.0).astype(BF16)


def _head_indicator_t(d, hs):
    return jnp.where(_iota((LANES, d), 1) // hs == _iota((LANES, d), 0), 1.0, 0.0).astype(BF16)


def _head_sum(x, hs):
    return _mm_sel(x, _head_indicator(x.shape[-1], hs))


def _head_bcast(s, d, hs):
    return _mm_sel(s, _head_indicator_t(d, hs))


def _row_tiles(B, T):
    tt = min(T, ROW_TILE)
    bb = max(1, min(B, ROW_TILE // tt))
    assert T % tt == 0 and B % bb == 0 and tt % 8 == 0
    return bb, tt


def _params(sem):
    return pltpu.CompilerParams(dimension_semantics=sem, vmem_limit_bytes=VMEM_LIMIT)


def _const_spec(shape):
    n = len(shape)
    return pl.BlockSpec(shape, lambda *_: (0,) * n)


def _prenorm(x, g, mod, d):
    shift = mod[:, :, 0:d]
    scale = mod[:, :, d:2 * d]
    y = x * lax.rsqrt(jnp.mean(x * x, axis=-1, keepdims=True) + NORM_EPS)
    return (y * g) * (1.0 + scale) + shift


def _mod_kernel(c_ref, w_ref, b_ref, o_ref):
    o_ref[0] = _mm1(_silu(c_ref[...]), w_ref[0]) + b_ref[0]


def _modulation(c, ada_w, ada_b):
    depth, d, n = ada_w.shape
    rows = c.shape[0]
    tn = 1536
    return pl.pallas_call(
        _mod_kernel,
        grid=(depth, n // tn),
        in_specs=[pl.BlockSpec((rows, d), lambda l, j: (0, 0)),
                  pl.BlockSpec((1, d, tn), lambda l, j: (l, 0, j)),
                  pl.BlockSpec((1, 1, tn), lambda l, j: (l, 0, j))],
        out_specs=pl.BlockSpec((1, rows, tn), lambda l, j: (l, 0, j)),
        out_shape=jax.ShapeDtypeStruct((depth, rows, n), F32),
        compiler_params=_params(("parallel", "parallel")),
        name="ada_modulation",
    )(c, ada_w, ada_b.reshape(depth, 1, n))


def _rwkv_pre_kernel(x_ref, halo_ref, xprev_ref, mod_ref, ng_ref, mix_ref, wr_ref, wk_ref, wv_ref, wg_ref,
                     w0_ref, w1_ref, w2_ref, a0_ref, a1_ref, a2_ref, kk_ref, ka_ref, rk_ref,
                     r_out, ld_out, k_out, v_out, kkn_out, a_out, g_out, bonus_out, shift_out):
    bb, tt, d = x_ref.shape
    m = bb * tt
    mod = mod_ref[...]
    g = ng_ref[...]
    h3 = _prenorm(x_ref[...], g, mod, d)
    h_halo = _prenorm(halo_ref[:, 7:8, :], g, mod, d)
    h_prev = jnp.where(pl.program_id(1) == 0, xprev_ref[...], h_halo)
    shift_out[...] = h3[:, tt - 1:tt, :]
    h = h3.reshape(m, d)
    prev = jnp.where(_iota((m, 1), 0) % tt == 0,
                     jnp.broadcast_to(h_prev, (bb, tt, d)).reshape(m, d),
                     pltpu.roll(h, 1, axis=0))
    xx = prev - h
    xr, xw, xk, xv, xa, xg = (h + xx * mix_ref[n:n + 1, :] for n in range(6))
    r = _mm1(xr, wr_ref[...])
    k = _mm1(xk, wk_ref[...])
    v = _mm1(xv, wv_ref[...])
    gate = _silu(_mm1(xg, wg_ref[...]))
    lw = w0_ref[...] + _mm1(jnp.tanh(_mm1(xw, w1_ref[...])), w2_ref[...])
    log_w = -_softplus(-lw) - 0.5
    a = _sigmoid(a0_ref[...] + _mm1(_mm1(xa, a1_ref[...]), a2_ref[...]))
    kk = k * kk_ref[...]
    nrm = jnp.maximum(jnp.sqrt(_head_sum(kk * kk, RW_HEAD)), 1e-12)
    kk = kk * _head_bcast(1.0 / nrm, d, RW_HEAD)
    k = k * (1.0 + (a - 1.0) * ka_ref[...])
    bonus = _head_bcast(_head_sum(r * k * rk_ref[...], RW_HEAD), d, RW_HEAD) * v
    for ref, val in ((r_out, r), (ld_out, -jnp.exp(log_w)), (k_out, k), (v_out, v), (kkn_out, kk),
                     (a_out, a), (g_out, gate), (bonus_out, bonus)):
        ref[...] = val.reshape(bb, tt, d)


def _pad_cols(w, n):
    return jnp.pad(w, ((0, 0), (0, n - w.shape[1])))


def _pad_rows(w, n):
    return jnp.pad(w, ((0, n - w.shape[0]), (0, 0)))


def _rwkv_pre(x, x_prev, mod, norm_g, p):
    B, T, d = x.shape
    tt = min(T, ROW_TILE // 2)
    bb = max(1, min(B, (ROW_TILE // 2) // tt))
    row = lambda a: a.reshape(1, d)
    tok = pl.BlockSpec((bb, tt, d), lambda i, j: (i, j, 0))
    per_seq = lambda n: pl.BlockSpec((bb, 1, n), lambda i, j: (i, 0, 0))
    halo = pl.BlockSpec((bb, 8, d), lambda i, j: (i, jnp.maximum(j * (tt // 8) - 1, 0), 0))
    consts = [row(norm_g), p['mix'],
              p['w_rkvg'][0].astype(BF16), p['w_rkvg'][1].astype(BF16), p['w_rkvg'][2].astype(BF16),
              p['w_rkvg'][3].astype(BF16),
              row(p['w0']), _pad_cols(p['w1'], LANES).astype(BF16), _pad_rows(p['w2'], LANES).astype(BF16),
              row(p['a0']), _pad_cols(p['a1'], LANES).astype(BF16), _pad_rows(p['a2'], LANES).astype(BF16),
              row(p['k_k']), row(p['k_a']), row(p['r_k'])]
    out = pl.pallas_call(
        _rwkv_pre_kernel,
        grid=(B // bb, T // tt),
        in_specs=[tok, halo, per_seq(d), per_seq(3 * d)] + [_const_spec(c.shape) for c in consts],
        out_specs=[tok] * 8 + [per_seq(d)],
        out_shape=[jax.ShapeDtypeStruct((B, T, d), F32)] * 8 + [jax.ShapeDtypeStruct((B, 1, d), F32)],
        compiler_params=_params(("parallel", "arbitrary")),
        name="rwkv_pre",
    )(x, x, x_prev.reshape(B, 1, d), mod, *consts)
    return out


def _unit_lower_inverse(m_strict, c):
    n = m_strict.shape[-1]
    ti = _iota((n, n), 0)
    si = _iota((n, n), 1)
    eye = jnp.where(ti == si, 1.0, 0.0)
    sub = min(SUB, c)
    d_part = jnp.where(ti // sub == si // sub, m_strict, 0.0)
    t_d = eye + d_part
    pw = d_part
    for _ in range(int(math.log2(sub)) - 1):
        pw = _mm3(pw, pw, _BNN)
        t_d = t_d + _mm3(t_d, pw, _BNN)
    if sub == c:
        return t_d
    assert c // sub == 4
    nn = _mm3(t_d, m_strict - d_part, _BNN)
    n2 = _mm3(nn, nn, _BNN)
    return _mm3(eye + nn + n2 + _mm3(nn, n2, _BNN), t_d, _BNN)


def _rwkv_chunk_kernel(r_ref, ld_ref, k_ref, v_ref, kk_ref, a_ref, s0_ref, o_ref, s_out_ref, s_scr, *, c):
    ns, tb, d = r_ref.shape
    n = ns * c
    pairs = d // LANES
    half = LANES // 2
    j = pl.program_id(1)

    @pl.when(j == 0)
    def _():
        z = jnp.zeros((half, half), F32)
        for s in range(ns):
            for p in range(pairs):
                top = jnp.concatenate([s0_ref[s, 2 * p], z], axis=1)
                bot = jnp.concatenate([z, s0_ref[s, 2 * p + 1]], axis=1)
                s_scr[s, p] = jnp.concatenate([top, bot], axis=0)

    lane_a = _iota((n, LANES), 1) < half
    ti = _iota((2 * n, 2 * n), 0)
    si = _iota((2 * n, 2 * n), 1)
    same = ti // c == si // c
    strict = same & (si < ti)
    incl = same & (si <= ti)
    tc = _iota((n, n), 0)
    sc = _iota((n, n), 1)
    tri = jnp.where((tc // c == sc // c) & (sc <= tc), 1.0, 0.0).astype(BF16)
    last = jnp.where((tc // c == sc // c) & (sc % c == c - 1), 1.0, 0.0).astype(BF16)
    bd = (_iota((LANES, LANES), 0) < half) == (_iota((LANES, LANES), 1) < half)

    def load(ref):
        x = ref[...]
        if tb < c:
            x = jnp.concatenate([x, jnp.zeros((ns, c - tb, d), F32)], axis=1)
        return x.reshape(n, d)

    def by_pair(x):
        return jnp.stack([x[:, p * LANES:(p + 1) * LANES] for p in range(pairs)], axis=0)

    def stack(x):
        return jnp.concatenate([jnp.where(lane_a, x, 0.0), jnp.where(lane_a, 0.0, x)], axis=1)

    def unstack(x):
        return x[:, 0:n] + x[:, n:2 * n]

    r, ld, k, v, kk, a = (load(ref) for ref in (r_ref, ld_ref, k_ref, v_ref, kk_ref, a_ref))
    cum = _sel_mm3(tri, ld)
    cum_last = _sel_mm3(last, cum)
    g_end = jnp.exp(cum_last - cum)
    g_inv = jnp.exp(-cum)
    beta = kk * a
    a_t = by_pair(-kk * jnp.exp(cum - ld))
    r_t = by_pair(r * jnp.exp(cum))
    k_t = by_pair(k * g_inv)
    b_t = by_pair(beta * g_inv)
    at_st = stack(a_t)
    rt_st = stack(r_t)
    k2 = jnp.concatenate([k_t, k_t], axis=1)
    b2 = jnp.concatenate([b_t, b_t], axis=1)
    m_k = jnp.where(strict, _mm1(at_st, k2, _BNT), 0.0)
    m_b = jnp.where(strict, _mm1(at_st, b2, _BNT), 0.0)
    a_k = jnp.where(incl, _mm1(rt_st, k2, _BNT), 0.0)
    a_b = jnp.where(incl, _mm1(rt_st, b2, _BNT), 0.0)
    t_inv = _unit_lower_inverse(m_b, c)
    v_p = by_pair(v)
    v_st = stack(v_p)
    states = [s_scr[s] for s in range(ns)]
    w1 = jnp.concatenate([_mm1(a_t[:, s * c:(s + 1) * c], states[s], _BNT) for s in range(ns)], axis=1)
    o1 = jnp.concatenate([_mm1(r_t[:, s * c:(s + 1) * c], states[s], _BNT) for s in range(ns)], axis=1)
    sa_st = _mm1(t_inv, stack(w1) + _mm1(m_k, v_st, _BNN), _BNN)
    o = o1 + unstack(_mm1(a_k, v_st, _BNN) + _mm1(a_b, sa_st, _BNN))
    sa = unstack(sa_st)
    for p in range(pairs):
        o_ref[:, :, p * LANES:(p + 1) * LANES] = o[p].reshape(ns, c, LANES)[:, 0:tb, :]
    kg = by_pair(k * g_end)
    bg = by_pair(beta * g_end)
    for s in range(ns):
        rows = slice(s * c, (s + 1) * c)
        upd = _mm1(v_p[:, rows], kg[:, rows], _BTN) + _mm1(sa[:, rows], bg[:, rows], _BTN)
        decay = by_pair(jnp.exp(cum_last[s * c:s * c + 1, :]))
        s_scr[s] = states[s] * decay + jnp.where(bd, upd, 0.0)

    @pl.when(j == pl.num_programs(1) - 1)
    def _():
        for s in range(ns):
            for p in range(pairs):
                blk = s_scr[s, p]
                s_out_ref[s, 2 * p] = blk[0:half, 0:half]
                s_out_ref[s, 2 * p + 1] = blk[half:LANES, half:LANES]


def _rwkv_chunk(r, ld, k, v, kk, a, s0):
    B, T, d = r.shape
    heads = d // RW_HEAD
    if T >= 64:
        c, ns, tb = 64, 1, 64
    else:
        c, ns, tb = 16, 4, T
    assert T % tb == 0 and B % ns == 0 and tb <= c
    tok = pl.BlockSpec((ns, tb, d), lambda i, j: (i, j, 0))
    st = pl.BlockSpec((ns, heads, RW_HEAD, RW_HEAD), lambda i, j: (i, 0, 0, 0))
    return pl.pallas_call(
        functools.partial(_rwkv_chunk_kernel, c=c),
        grid=(B // ns, T // tb),
        in_specs=[tok] * 6 + [st],
        out_specs=[tok, st],
        out_shape=[jax.ShapeDtypeStruct((B, T, d), F32),
                   jax.ShapeDtypeStruct((B, heads, RW_HEAD, RW_HEAD), F32)],
        scratch_shapes=[pltpu.VMEM((ns, d // LANES, LANES, LANES), F32)],
        compiler_params=_params(("parallel", "arbitrary")),
        name="rwkv_chunk",
    )(r, ld, k, v, kk, a, s0)


def _residual_out(y, w_ref, x_ref, mod_ref, fg_ref, o_ref, final):
    bb, tt, d = x_ref.shape
    out = _mm1(y, w_ref[...]).reshape(bb, tt, d)
    x = x_ref[...] + mod_ref[:, :, 2 * d:3 * d] * out
    if final:
        x = x * lax.rsqrt(jnp.mean(x * x, axis=-1, keepdims=True) + NORM_EPS) * fg_ref[...]
    o_ref[...] = x


def _rwkv_post_kernel(o_ref, bonus_ref, g_ref, x_ref, mod_ref, gnw_ref, gnb_ref, wo_ref, fg_ref, out_ref, *, final):
    bb, tt, d = x_ref.shape
    m = bb * tt
    o = o_ref[...].reshape(m, d)
    mu = _head_bcast(_head_sum(o, RW_HEAD) * (1.0 / RW_HEAD), d, RW_HEAD)
    oc = o - mu
    var = _head_bcast(_head_sum(oc * oc, RW_HEAD) * (1.0 / RW_HEAD), d, RW_HEAD)
    y = oc * lax.rsqrt(var + RW_GN_EPS) * gnw_ref[...] + gnb_ref[...]
    y = (y + bonus_ref[...].reshape(m, d)) * g_ref[...].reshape(m, d)
    _residual_out(y, wo_ref, x_ref, mod_ref, fg_ref, out_ref, final)


def _post_call(kernel, name, x, mod, tok_inputs, consts, final):
    B, T, d = x.shape
    bb, tt = _row_tiles(B, T)
    tok = lambda a: pl.BlockSpec((bb, tt, a.shape[-1]), lambda i, j: (i, j, 0))
    return pl.pallas_call(
        functools.partial(kernel, final=final),
        grid=(B // bb, T // tt),
        in_specs=[tok(a) for a in tok_inputs] + [tok(x), pl.BlockSpec((bb, 1, 3 * d), lambda i, j: (i, 0, 0))]
        + [_const_spec(c.shape) for c in consts],
        out_specs=tok(x),
        out_shape=jax.ShapeDtypeStruct((B, T, d), F32),
        compiler_params=_params(("parallel", "parallel")),
        name=name,
    )(*tok_inputs, x, mod, *consts)


def _rwkv_layer(x, mod, norm_g, final_g, final, x_prev, s0, p):
    d = x.shape[-1]
    r, ld, k, v, kk, a, g, bonus, shift = _rwkv_pre(x, x_prev, mod, norm_g, p)
    o, s_new = _rwkv_chunk(r, ld, k, v, kk, a, s0)
    consts = [p['gn_w'].reshape(1, d), p['gn_b'].reshape(1, d), p['w_o'].astype(BF16), final_g.reshape(1, d)]
    x = _post_call(_rwkv_post_kernel, "rwkv_post", x, mod, [o, bonus, g], consts, final)
    return x, s_new, shift[:, 0, :]


def _gla_pre_kernel(x_ref, mod_ref, ng_ref, w_ref, wl_ref, w2_ref, b2_ref, q_out, k_out, v_out, gate_out, gk_out,
                    *, qk):
    bb, tt, d = x_ref.shape
    m = bb * tt
    h = _prenorm(x_ref[...], ng_ref[...], mod_ref[...], d).reshape(m, d).astype(BF16)
    dk = qk // GLA_HEADS
    q_out[...] = (_dot(h, w_ref[:, 0:qk]) * dk ** -0.5).reshape(bb, tt, qk)
    k_out[...] = _dot(h, w_ref[:, qk:2 * qk]).reshape(bb, tt, qk)
    v_out[...] = _dot(h, w_ref[:, 2 * qk:2 * qk + d]).reshape(bb, tt, d)
    gate_out[...] = _dot(h, w_ref[:, 2 * qk + d:2 * qk + 2 * d]).reshape(bb, tt, d)
    low = _dot(h, wl_ref[...])
    gk = _mm3(low, w2_ref[...]) + b2_ref[...]
    gk_out[...] = (-_softplus(-gk) * (1.0 / GLA_GATE_NORMALIZER)).reshape(bb, tt, qk)


def _hg_pre_kernel(x_ref, mod_ref, ng_ref, w_ref, lower_ref, q_out, k_out, v_out, gate_out, g_out, *, layer):
    bb, tt, d = x_ref.shape
    m = bb * tt
    h = _prenorm(x_ref[...], ng_ref[...], mod_ref[...], d).reshape(m, d).astype(BF16)
    dk = d // HG_HEADS
    low = lower_ref[...]
    e = jnp.exp(low - jnp.max(low, axis=0, keepdims=True))
    soft = e / jnp.sum(e, axis=0, keepdims=True)
    lb = jnp.sum(soft[0:layer + 1], axis=0, keepdims=True) - soft[0:1]
    q = _dot(h, w_ref[:, 0:d])
    f = _dot(h, w_ref[:, d:2 * d])
    forget = lb + (1.0 - lb) * _sigmoid(f)
    q_out[...] = (_silu(q) * dk ** -0.5).reshape(bb, tt, d)
    k_out[...] = (1.0 - forget).reshape(bb, tt, d)
    g_out[...] = jnp.log(forget).reshape(bb, tt, d)
    v_out[...] = _dot(h, w_ref[:, 2 * d:3 * d]).reshape(bb, tt, d)
    gate_out[...] = _dot(h, w_ref[:, 3 * d:4 * d]).reshape(bb, tt, d)


def _pre_call(kernel, name, x, mod, consts, out_widths):
    B, T, d = x.shape
    bb, tt = _row_tiles(B, T)
    tok = lambda n: pl.BlockSpec((bb, tt, n), lambda i, j: (i, j, 0))
    return pl.pallas_call(
        kernel,
        grid=(B // bb, T // tt),
        in_specs=[tok(d), pl.BlockSpec((bb, 1, 3 * d), lambda i, j: (i, 0, 0))]
        + [_const_spec(c.shape) for c in consts],
        out_specs=[tok(n) for n in out_widths],
        out_shape=[jax.ShapeDtypeStruct((B, T, n), F32) for n in out_widths],
        compiler_params=_params(("parallel", "parallel")),
        name=name,
    )(x, mod, *consts)


def _gla_chunk_kernel(q_ref, k_ref, g_ref, v_ref, s0_ref, o_ref, s_out_ref, st_scr, *, c, heads):
    tb = q_ref.shape[1]
    dk = q_ref.shape[2] // heads
    dv = v_ref.shape[2] // heads
    j = pl.program_id(1)

    @pl.when(j == 0)
    def _():
        for h in range(heads):
            st_scr[h] = s0_ref[0, h].T

    def load(ref, w):
        x = ref[0]
        if tb < c:
            x = jnp.concatenate([x, jnp.zeros((c - tb, x.shape[1]), F32)], axis=0)
        return jnp.stack([x[:, h * w:(h + 1) * w] for h in range(heads)], axis=0)

    q, k, g, v = load(q_ref, dk), load(k_ref, dk), load(g_ref, dk), load(v_ref, dv)
    tc = _iota((c, c), 0)
    sc = _iota((c, c), 1)
    tri = jnp.where(sc <= tc, 1.0, 0.0).astype(BF16)
    b = jnp.stack([_sel_mm3(tri, g[h]) for h in range(heads)], axis=0)
    st = st_scr[...]
    o_inter = _mm3(q * jnp.exp(b), st, _BNT)
    sub = min(GLA_SUB, c)
    key_row = _iota((c, 1), 0)
    row = _iota((sub, 1), 0)
    outs = []
    for i in range(c // sub):
        r0 = i * sub
        qi = q[:, r0:r0 + sub]
        bi = b[:, r0:r0 + sub]
        acc = o_inter[:, r0:r0 + sub]
        if i > 0:
            ref_b = b[:, r0 - 1:r0]
            k_fac = jnp.exp(jnp.where(key_row < r0, ref_b - b, -jnp.inf))
            att = _mm3(qi * jnp.exp(bi - ref_b), k * k_fac, _BNT)
            acc = acc + _mm3(att, v, _BNN)
        for s in range(sub):
            diff = jnp.where(row >= s, bi - bi[:, s:s + 1], -jnp.inf)
            col = jnp.sum(qi * k[:, r0 + s:r0 + s + 1] * jnp.exp(diff), axis=2, keepdims=True)
            acc = acc + col * v[:, r0 + s:r0 + s + 1]
        outs.append(acc)
    o = jnp.concatenate(outs, axis=1) if len(outs) > 1 else outs[0]
    for h in range(heads):
        o_ref[0, :, h * dv:(h + 1) * dv] = o[h, 0:tb]
    b_last = b[:, c - 1:c]
    st_new = st * jnp.exp(b_last) + _mm3(v, k * jnp.exp(b_last - b), _BTN)
    st_scr[...] = st_new

    @pl.when(j == pl.num_programs(1) - 1)
    def _():
        for h in range(heads):
            s_out_ref[0, h] = st_new[h].T


def _gla_chunk(q, k, g, v, s0, heads):
    B, T, qk = q.shape
    d = v.shape[-1]
    dk = qk // heads
    dv = d // heads
    tb = 64 if T % 64 == 0 else T
    c = -(-tb // SUB) * SUB
    kspec = pl.BlockSpec((1, tb, qk), lambda b, j: (b, j, 0))
    vspec = pl.BlockSpec((1, tb, d), lambda b, j: (b, j, 0))
    sspec = pl.BlockSpec((1, heads, dk, dv), lambda b, j: (b, 0, 0, 0))
    return pl.pallas_call(
        functools.partial(_gla_chunk_kernel, c=c, heads=heads),
        grid=(B, T // tb),
        in_specs=[kspec, kspec, kspec, vspec, sspec],
        out_specs=[vspec, sspec],
        out_shape=[jax.ShapeDtypeStruct((B, T, d), F32),
                   jax.ShapeDtypeStruct((B, heads, dk, dv), F32)],
        scratch_shapes=[pltpu.VMEM((heads, dv, dk), F32)],
        compiler_params=_params(("parallel", "arbitrary")),
        name="gla_chunk",
    )(q, k, g, v, s0)


def _gla_post_kernel(o_ref, gate_ref, x_ref, mod_ref, gnw_ref, wo_ref, fg_ref, out_ref, *, final, heads):
    bb, tt, d = x_ref.shape
    m = bb * tt
    hs = d // heads
    o = o_ref[...].reshape(m, d)
    ms = _head_bcast(_head_sum(o * o, hs) * (1.0 / hs), d, hs)
    y = o * lax.rsqrt(ms + NORM_EPS) * gnw_ref[...]
    y = y * _silu(gate_ref[...].reshape(m, d))
    _residual_out(y, wo_ref, x_ref, mod_ref, fg_ref, out_ref, final)


def _gla_layer(x, mod, norm_g, final_g, final, s0, p):
    d = x.shape[-1]
    qk = p['w_gk2'].shape[1]
    w_in = p['w_in']
    consts = [norm_g.reshape(1, d), w_in[:, :2 * qk + 2 * d].astype(BF16),
              _pad_cols(w_in[:, 2 * qk + 2 * d:], LANES).astype(BF16), _pad_rows(p['w_gk2'], LANES),
              p['b_gk2'].reshape(1, qk)]
    q, k, v, gate, gk = _pre_call(functools.partial(_gla_pre_kernel, qk=qk), "gla_pre", x, mod, consts,
                                  (qk, qk, d, d, qk))
    o, s_new = _gla_chunk(q, k, gk, v, s0, GLA_HEADS)
    post_consts = [jnp.tile(p['gn_w'], GLA_HEADS).reshape(1, d), p['w_o'].astype(BF16), final_g.reshape(1, d)]
    x = _post_call(functools.partial(_gla_post_kernel, heads=GLA_HEADS), "gla_post", x, mod, [o, gate], post_consts,
                   final)
    return x, s_new


def _hg_layer(x, mod, norm_g, final_g, final, s0, layer, p):
    d = x.shape[-1]
    consts = [norm_g.reshape(1, d), p['w_in'].astype(BF16), p['lower']]
    q, k, v, gate, g = _pre_call(functools.partial(_hg_pre_kernel, layer=layer), "hgrn_pre", x, mod, consts,
                                 (d, d, d, d, d))
    o, s_new = _gla_chunk(q, k, g, v, s0, HG_HEADS)
    post_consts = [jnp.tile(p['gn_w'], HG_HEADS).reshape(1, d), p['w_o'].astype(BF16), final_g.reshape(1, d)]
    x = _post_call(functools.partial(_gla_post_kernel, heads=HG_HEADS), "hgrn_post", x, mod, [o, gate], post_consts,
                   final)
    return x, s_new


def _sb_pre_kernel(x_ref, mod_ref, ng_ref, w_ref, q_out, k_out, v_out, gate_out):
    bb, tt, d = x_ref.shape
    m = bb * tt
    h = _prenorm(x_ref[...], ng_ref[...], mod_ref[...], d).reshape(m, d).astype(BF16)
    hd = d // SB_HEADS
    q_out[...] = (_dot(h, w_ref[:, 0:d]) * hd ** -0.5).reshape(bb, tt, d)
    k_out[...] = _dot(h, w_ref[:, d:2 * d]).reshape(bb, tt, d)
    v_out[...] = _dot(h, w_ref[:, 2 * d:3 * d]).reshape(bb, tt, d)
    gate_out[...] = _dot(h, w_ref[:, 3 * d:4 * d]).reshape(bb, tt, d)


def _sb_prompt_kernel(bias_ref, q_ref, k_ref, v_ref, o_ref, acc_scr, z_scr, sp_scr, *, tq):
    p = pl.program_id(1)
    i = pl.program_id(2)
    half = LANES // 2
    q = q_ref[0] * LOG2E
    lane_a = _iota((tq, LANES), 1) < half
    q_st = jnp.concatenate([jnp.where(lane_a, q, 0.0), jnp.where(lane_a, 0.0, q)], axis=0).astype(BF16)
    bias = jnp.where(_iota((2 * tq, 1), 0) < tq, bias_ref[2 * p], bias_ref[2 * p + 1]) * LOG2E
    tri = jnp.where(_iota((tq, tq), 0) >= _iota((tq, tq), 1), 1.0, 0.0).astype(BF16)

    def block_rows(jb):
        return pl.ds(pl.multiple_of(jnp.maximum(jb, 0) * tq, tq), tq)

    def scores(jb):
        z2 = _dot(q_st, k_ref[0, block_rows(jb), :].astype(BF16), _NT) + bias
        sp2 = jnp.maximum(z2, 0.0) + jnp.log2(1.0 + jnp.exp2(-jnp.abs(z2)))
        return z2, sp2

    def score_into(slot, jb):
        z2, sp2 = scores(jb)
        z_scr[slot] = z2
        sp_scr[slot] = sp2.astype(BF16)

    def finish(slot, jb, carry):
        cs = _dot(sp_scr[slot], tri)
        w = jnp.exp2(z_scr[slot] - cs - carry)
        acc_scr[...] += _dot(w.astype(BF16), v_ref[0, block_rows(jb), :].astype(BF16))
        return carry + cs[:, 0:1]

    score_into(0, i - 1)

    z2, sp2 = scores(i)
    before = _iota((2 * tq, tq), 1) < _iota((2 * tq, tq), 0) % tq
    spm = jnp.where(before, sp2, 0.0).astype(BF16)
    cs = _dot(spm, tri)
    w = jnp.where(before, jnp.exp2((z2 - sp2) - (cs - spm.astype(F32))), 0.0)
    acc_scr[...] = _dot(w.astype(BF16), v_ref[0, block_rows(i), :].astype(BF16))
    carry = cs[:, 0:1]

    def pair(n, carry):
        jb = i - 1 - 2 * n
        score_into(1, jb - 1)
        carry = finish(0, jb, carry)
        score_into(0, jb - 2)
        return finish(1, jb - 1, carry)

    carry = lax.fori_loop(0, i // 2, pair, carry)

    @pl.when(i % 2 == 1)
    def _():
        finish(0, 0, carry)

    acc = acc_scr[...]
    o_ref[0] = jnp.where(lane_a, acc[0:tq], acc[tq:2 * tq])


def _sb_prompt(q, k, v, bias):
    B, T, d = q.shape
    tq = min(T, 256)
    qspec = pl.BlockSpec((1, tq, LANES), lambda b, p, i: (b, i, p))
    kspec = pl.BlockSpec((1, T, LANES), lambda b, p, i: (b, 0, p))
    return pl.pallas_call(
        functools.partial(_sb_prompt_kernel, tq=tq),
        grid=(B, d // LANES, T // tq),
        in_specs=[pl.BlockSpec(memory_space=pltpu.SMEM), qspec, kspec, kspec],
        out_specs=qspec,
        out_shape=jax.ShapeDtypeStruct((B, T, d), F32),
        scratch_shapes=[pltpu.VMEM((2 * tq, LANES), F32), pltpu.VMEM((2, 2 * tq, tq), F32),
                        pltpu.VMEM((2, 2 * tq, tq), BF16)],
        compiler_params=_params(("parallel", "parallel", "arbitrary")),
        name="sb_attention_prompt",
    )(bias, q, k, v)


def _sb_paged_kernel(pt_ref, q_ref, kn_ref, vn_ref, bias_ref, *refs, n_blocks):
    kc_refs, vc_refs = refs[:n_blocks], refs[n_blocks:2 * n_blocks]
    o_ref, acc_scr, carry_scr = refs[2 * n_blocks:]
    t, d = q_ref.shape[1], q_ref.shape[2]
    heads = LANES // t
    hd = d // heads
    j = pl.program_id(1)
    q = q_ref[0] * LOG2E
    q_h = [q[:, h * hd:(h + 1) * hd].astype(BF16) for h in range(heads)]
    bias = bias_ref[...] * LOG2E

    def attend(k_of_head, v_of_head, nk, before, carry):
        z2 = jnp.concatenate([_dot(q_h[h], k_of_head(h), _NT) for h in range(heads)], axis=0) + bias[:, 0:nk]
        sp2 = jnp.maximum(z2, 0.0) + jnp.log2(1.0 + jnp.exp2(-jnp.abs(z2)))
        spb = (sp2 if before is None else jnp.where(before, sp2, 0.0)).astype(BF16)
        tri = jnp.where(_iota((nk, nk), 0) >= _iota((nk, nk), 1), 1.0, 0.0).astype(BF16)
        cs = _dot(spb, tri)
        if before is None:
            w = jnp.exp2(z2 - cs - carry)
        else:
            w = jnp.where(before, jnp.exp2((z2 - sp2) - (cs - spb.astype(F32))), 0.0)
        wb = w.astype(BF16)
        upd = jnp.concatenate([_dot(wb[h * t:(h + 1) * t], v_of_head(h)) for h in range(heads)], axis=0)
        total = _dot(spb, jnp.ones((nk, LANES), BF16))
        return upd, total

    @pl.when(j == 0)
    def _():
        nk = 2 * t
        pad = jnp.zeros((nk - t, hd), F32)
        kn, vn = kn_ref[0], vn_ref[0]
        before = _iota((LANES, nk), 1) < _iota((LANES, nk), 0) % t
        upd, total = attend(lambda h: jnp.concatenate([kn[:, h * hd:(h + 1) * hd], pad], axis=0).astype(BF16),
                            lambda h: jnp.concatenate([vn[:, h * hd:(h + 1) * hd], pad], axis=0).astype(BF16),
                            nk, before, None)
        acc_scr[...] = upd
        carry_scr[...] = total

    carry = carry_scr[...]
    acc = acc_scr[...]
    for kc_ref, vc_ref in zip(kc_refs, vc_refs):
        page = kc_ref.shape[0]
        k_rows = kc_ref.reshape(page * heads, hd)
        v_rows = vc_ref.reshape(page * heads, hd)
        upd, total = attend(lambda h: k_rows[pl.ds(h, page, stride=heads), :].astype(BF16),
                            lambda h: v_rows[pl.ds(h, page, stride=heads), :].astype(BF16), page, None, carry)
        acc = acc + upd
        carry = carry + total
    acc_scr[...] = acc
    carry_scr[...] = carry

    @pl.when(j == pl.num_programs(1) - 1)
    def _():
        o_ref[0] = jnp.concatenate([acc[h * t:(h + 1) * t, :] for h in range(heads)], axis=1)


def _sb_paged(q, k_new, v_new, bias, k_cache, v_cache, layer, page_table):
    B, T, d = q.shape
    _, _, page, heads, hd = k_cache.shape
    n_pages = page_table.shape[1]
    n_blocks = 4 if n_pages % 4 == 0 else 1
    assert heads * T == LANES and heads * hd == d
    tok = pl.BlockSpec((1, T, d), lambda b, j, pt: (b, 0, 0))

    def cache(i):
        return pl.BlockSpec((None, None, page, heads, hd),
                            lambda b, j, pt: (layer, pt[b, n_pages - 1 - (j * n_blocks + i)], 0, 0, 0))

    bias_rows = jnp.broadcast_to(jnp.repeat(bias, T)[:, None], (LANES, LANES))
    grid_spec = pltpu.PrefetchScalarGridSpec(
        num_scalar_prefetch=1,
        grid=(B, n_pages // n_blocks),
        in_specs=[tok, tok, tok, pl.BlockSpec((LANES, LANES), lambda b, j, pt: (0, 0))]
        + [cache(i) for i in range(n_blocks)] * 2,
        out_specs=tok,
        scratch_shapes=[pltpu.VMEM((LANES, hd), F32), pltpu.VMEM((LANES, LANES), F32)],
    )
    return pl.pallas_call(
        functools.partial(_sb_paged_kernel, n_blocks=n_blocks),
        grid_spec=grid_spec,
        out_shape=jax.ShapeDtypeStruct((B, T, d), F32),
        compiler_params=_params(("parallel", "arbitrary")),
        name="sb_attention_paged",
    )(page_table, q, k_new, v_new, bias_rows, *([k_cache] * n_blocks), *([v_cache] * n_blocks))


def _sb_post_kernel(o_ref, gate_ref, x_ref, mod_ref, wo_ref, fg_ref, out_ref, *, final):
    bb, tt, d = x_ref.shape
    m = bb * tt
    y = o_ref[...].reshape(m, d) * _silu(gate_ref[...].reshape(m, d))
    _residual_out(y, wo_ref, x_ref, mod_ref, fg_ref, out_ref, final)


def _sb_layer(x, mod, norm_g, final_g, final, past, p):
    B, T, d = x.shape
    consts = [norm_g.reshape(1, d), p['w_in'].astype(BF16)]
    q, k, v, gate = _pre_call(_sb_pre_kernel, "sb_pre", x, mod, consts, (d, d, d, d))
    if past is None:
        o = _sb_prompt(q, k, v, p['bias'])
    else:
        o = _sb_paged(q, k, v, p['bias'], *past)
    x = _post_call(_sb_post_kernel, "sb_post", x, mod, [o, gate], [p['w_o'].astype(BF16), final_g.reshape(1, d)],
                   final)
    hd = d // SB_HEADS
    return x, k.reshape(B, T, SB_HEADS, hd), v.reshape(B, T, SB_HEADS, hd)


def kernel(x_prompt, x_sample, c_prompt, c_sample, state_rwkv, cache_rwkv_shift, state_gla, cache_sb_k, cache_sb_v, state_hgrn, page_table, norm_g, ada_w, ada_b, final_g, rw_mix, rw_w_rkvg, rw_w0, rw_w1, rw_w2, rw_a0, rw_a1, rw_a2, rw_k_k, rw_k_a, rw_r_k, rw_gn_w, rw_gn_b, rw_w_o, gla_w_in, gla_w_gk2, gla_b_gk2, gla_gn_w, gla_w_o, sb_w_in, sb_bias, sb_w_o, hg_w_in, hg_lower, hg_gn_w, hg_w_o):
    depth, d = norm_g.shape
    n_mix = 4
    bp, bs = x_prompt.shape[0], x_sample.shape[0]
    rows = -(-(bp + bs) // 8) * 8
    c_all = jnp.pad(jnp.concatenate([c_prompt, c_sample], axis=0), ((0, rows - bp - bs), (0, 0)))
    mods = _modulation(c_all, ada_w, ada_b)

    def trunk(x, mod_all, rw_s, rw_shift, gla_s, sb_past, hg_s):
        B = x.shape[0]
        outs = {n: [] for n in ('rw_s', 'rw_shift', 'gla_s', 'sb_k', 'sb_v', 'hg_s')}
        for i in range(depth):
            kind, j = i % n_mix, i // n_mix
            mod = mod_all[i].reshape(B, 1, 3 * d)
            final = i == depth - 1
            if kind == 0:
                p = dict(mix=rw_mix[j], w_rkvg=rw_w_rkvg[j], w0=rw_w0[j], w1=rw_w1[j], w2=rw_w2[j], a0=rw_a0[j],
                         a1=rw_a1[j], a2=rw_a2[j], k_k=rw_k_k[j], k_a=rw_k_a[j], r_k=rw_r_k[j], gn_w=rw_gn_w[j],
                         gn_b=rw_gn_b[j], w_o=rw_w_o[j])
                x, s, sh = _rwkv_layer(x, mod, norm_g[i], final_g, final, rw_shift[j], rw_s[j], p)
                outs['rw_s'].append(s)
                outs['rw_shift'].append(sh)
            elif kind == 1:
                p = dict(w_in=gla_w_in[j], w_gk2=gla_w_gk2[j], b_gk2=gla_b_gk2[j], gn_w=gla_gn_w[j], w_o=gla_w_o[j])
                x, s = _gla_layer(x, mod, norm_g[i], final_g, final, gla_s[j], p)
                outs['gla_s'].append(s)
            elif kind == 2:
                p = dict(w_in=sb_w_in[j], bias=sb_bias[j], w_o=sb_w_o[j])
                x, k_new, v_new = _sb_layer(x, mod, norm_g[i], final_g, final, sb_past(j), p)
                outs['sb_k'].append(k_new)
                outs['sb_v'].append(v_new)
            else:
                p = dict(w_in=hg_w_in[j], lower=hg_lower, gn_w=hg_gn_w[j], w_o=hg_w_o[j])
                x, s = _hg_layer(x, mod, norm_g[i], final_g, final, hg_s[j], i, p)
                outs['hg_s'].append(s)
        st = lambda n: jnp.stack(outs[n])
        return x, st('rw_s'), st('rw_shift'), st('gla_s'), st('sb_k'), st('sb_v'), st('hg_s')

    z = lambda a, b: jnp.zeros((a.shape[0], b) + a.shape[2:], a.dtype)
    prompt = trunk(x_prompt, mods[:, :bp], z(state_rwkv, bp), z(cache_rwkv_shift, bp), z(state_gla, bp),
                   lambda j: None, z(state_hgrn, bp))
    sample = trunk(x_sample, mods[:, bp:bp + bs], state_rwkv, cache_rwkv_shift, state_gla,
                   lambda j: (cache_sb_k, cache_sb_v, j, page_table), state_hgrn)
    return (prompt[0], sample[0]) + prompt[1:] + sample[1:]
```

```python
import functools
import math

import jax
import jax.numpy as jnp
from jax import lax
from jax.experimental import pallas as pl
from jax.experimental.pallas import tpu as pltpu

F32 = jnp.float32
BF16 = jnp.bfloat16

NORM_EPS = 1e-6
RW_HEAD = 64
RW_GN_EPS = 64e-5
GLA_HEADS = 4
GLA_GATE_NORMALIZER = 16.0
SB_HEADS = 16
HG_HEADS = 8
LANES = 128
ROW_TILE = 512
SUB = 16
GLA_SUB = 16
VMEM_LIMIT = 56 * 1024 * 1024
LOG2E = 1.4426950408889634

_NN = (((1,), (0,)), ((), ()))
_NT = (((1,), (1,)), ((), ()))
_TN = (((0,), (0,)), ((), ()))
_BNN = (((2,), (1,)), ((0,), (0,)))
_BNT = (((2,), (2,)), ((0,), (0,)))
_BTN = (((1,), (1,)), ((0,), (0,)))


def _dot(a, b, dims=_NN):
    return lax.dot_general(a, b, dims, preferred_element_type=F32)


def _split(x):
    hi = x.astype(BF16)
    lo = (x - hi.astype(F32)).astype(BF16)
    return hi, lo


def _mm1(a, b, dims=_NN):
    return _dot(a.astype(BF16), b.astype(BF16), dims)


def _mm3(a, b, dims=_NN):
    ah, al = _split(a)
    bh, bl = _split(b)
    return _dot(ah, bh, dims) + (_dot(ah, bl, dims) + _dot(al, bh, dims))


def _mm_sel(a, sel, dims=_NN):
    ah, al = _split(a)
    return _dot(ah, sel, dims) + _dot(al, sel, dims)


def _mm_sel_fused(a, sel2):
    ah, al = _split(a)
    return _dot(jnp.concatenate([ah, al], axis=1), sel2)


def _sel_mm(sel, b, dims=_NN):
    bh, bl = _split(b)
    return _dot(sel, bh, dims) + _dot(sel, bl, dims)


def _sel_mm3(sel, b):
    b1 = b.astype(BF16)
    r1 = b - b1.astype(F32)
    b2 = r1.astype(BF16)
    b3 = (r1 - b2.astype(F32)).astype(BF16)
    return _dot(sel, b1) + (_dot(sel, b2) + _dot(sel, b3))


def _silu(x):
    return x * (1.0 / (1.0 + jnp.exp(-x)))


def _sigmoid(x):
    return 1.0 / (1.0 + jnp.exp(-x))


def _softplus(x):
    return jnp.maximum(x, 0.0) + jnp.log1p(jnp.exp(-jnp.abs(x)))


def _iota(shape, dim):
    return lax.broadcasted_iota(jnp.int32, shape, dim)


def _head_indicator(d, hs):
    return jnp.where(_iota((d, LANES), 0) // hs == _iota((d, LANES), 1), 1.0, 0.0).astype(BF16)


def _head_indicator_t(d, hs):
    return jnp.where(_iota((LANES, d), 1) // hs == _iota((LANES, d), 0), 1.0, 0.0).astype(BF16)


def _head_sum(x, hs):
    return _mm_sel(x, _head_indicator(x.shape[-1], hs))


def _head_bcast(s, d, hs):
    return _mm_sel(s, _head_indicator_t(d, hs))


def _row_tiles(B, T):
    tt = min(T, ROW_TILE)
    bb = max(1, min(B, ROW_TILE // tt))
    assert T % tt == 0 and B % bb == 0 and tt % 8 == 0
    return bb, tt


def _params(sem):
    return pltpu.CompilerParams(dimension_semantics=sem, vmem_limit_bytes=VMEM_LIMIT)


def _const_spec(shape):
    n = len(shape)
    return pl.BlockSpec(shape, lambda *_: (0,) * n)


def _prenorm(x, g, mod, d):
    shift = mod[:, :, 0:d]
    scale = mod[:, :, d:2 * d]
    y = x * lax.rsqrt(jnp.mean(x * x, axis=-1, keepdims=True) + NORM_EPS)
    return (y * g) * (1.0 + scale) + shift


def _mod_kernel(c_ref, w_ref, b_ref, o_ref):
    o_ref[0] = _mm1(_silu(c_ref[...]), w_ref[0]) + b_ref[0]


def _modulation(c, ada_w, ada_b):
    depth, d, n = ada_w.shape
    rows = c.shape[0]
    tn = 1536
    return pl.pallas_call(
        _mod_kernel,
        grid=(depth, n // tn),
        in_specs=[pl.BlockSpec((rows, d), lambda l, j: (0, 0)),
                  pl.BlockSpec((1, d, tn), lambda l, j: (l, 0, j)),
                  pl.BlockSpec((1, 1, tn), lambda l, j: (l, 0, j))],
        out_specs=pl.BlockSpec((1, rows, tn), lambda l, j: (l, 0, j)),
        out_shape=jax.ShapeDtypeStruct((depth, rows, n), F32),
        compiler_params=_params(("parallel", "parallel")),
        name="ada_modulation",
    )(c, ada_w, ada_b.reshape(depth, 1, n))


def _rwkv_pre_kernel(x_ref, halo_ref, xprev_ref, mod_ref, ng_ref, mix_ref, wr_ref, wk_ref, wv_ref, wg_ref,
                     w0_ref, w1_ref, w2_ref, a0_ref, a1_ref, a2_ref, kk_ref, ka_ref, rk_ref,
                     r_out, ld_out, k_out, v_out, kkn_out, a_out, g_out, bonus_out, shift_out):
    bb, tt, d = x_ref.shape
    m = bb * tt
    mod = mod_ref[...]
    g = ng_ref[...]
    h3 = _prenorm(x_ref[...], g, mod, d)
    h_halo = _prenorm(halo_ref[:, 7:8, :], g, mod, d)
    h_prev = jnp.where(pl.program_id(1) == 0, xprev_ref[...], h_halo)
    shift_out[...] = h3[:, tt - 1:tt, :]
    h = h3.reshape(m, d)
    prev = jnp.where(_iota((m, 1), 0) % tt == 0,
                     jnp.broadcast_to(h_prev, (bb, tt, d)).reshape(m, d),
                     pltpu.roll(h, 1, axis=0))
    xx = prev - h
    xr, xw, xk, xv, xa, xg = (h + xx * mix_ref[n:n + 1, :] for n in range(6))
    r = _mm1(xr, wr_ref[...])
    k = _mm1(xk, wk_ref[...])
    v = _mm1(xv, wv_ref[...])
    gate = _silu(_mm1(xg, wg_ref[...]))
    lw = w0_ref[...] + _mm1(jnp.tanh(_mm1(xw, w1_ref[...])), w2_ref[...])
    log_w = -_softplus(-lw) - 0.5
    a = _sigmoid(a0_ref[...] + _mm1(_mm1(xa, a1_ref[...]), a2_ref[...]))
    kk = k * kk_ref[...]
    nrm = jnp.maximum(jnp.sqrt(_head_sum(kk * kk, RW_HEAD)), 1e-12)
    kk = kk * _head_bcast(1.0 / nrm, d, RW_HEAD)
    k = k * (1.0 + (a - 1.0) * ka_ref[...])
    bonus = _head_bcast(_head_sum(r * k * rk_ref[...], RW_HEAD), d, RW_HEAD) * v
    for ref, val in ((r_out, r), (ld_out, -jnp.exp(log_w)), (k_out, k), (v_out, v), (kkn_out, kk),
                     (a_out, a), (g_out, gate), (bonus_out, bonus)):
        ref[...] = val.reshape(bb, tt, d)


def _pad_cols(w, n):
    return jnp.pad(w, ((0, 0), (0, n - w.shape[1])))


def _pad_rows(w, n):
    return jnp.pad(w, ((0, n - w.shape[0]), (0, 0)))


def _rwkv_pre(x, x_prev, mod, norm_g, p):
    B, T, d = x.shape
    tt = min(T, ROW_TILE // 2)
    bb = max(1, min(B, (ROW_TILE // 2) // tt))
    row = lambda a: a.reshape(1, d)
    tok = pl.BlockSpec((bb, tt, d), lambda i, j: (i, j, 0))
    per_seq = lambda n: pl.BlockSpec((bb, 1, n), lambda i, j: (i, 0, 0))
    halo = pl.BlockSpec((bb, 8, d), lambda i, j: (i, jnp.maximum(j * (tt // 8) - 1, 0), 0))
    consts = [row(norm_g), p['mix'],
              p['w_rkvg'][0].astype(BF16), p['w_rkvg'][1].astype(BF16), p['w_rkvg'][2].astype(BF16),
              p['w_rkvg'][3].astype(BF16),
              row(p['w0']), _pad_cols(p['w1'], LANES).astype(BF16), _pad_rows(p['w2'], LANES).astype(BF16),
              row(p['a0']), _pad_cols(p['a1'], LANES).astype(BF16), _pad_rows(p['a2'], LANES).astype(BF16),
              row(p['k_k']), row(p['k_a']), row(p['r_k'])]
    out = pl.pallas_call(
        _rwkv_pre_kernel,
        grid=(B // bb, T // tt),
        in_specs=[tok, halo, per_seq(d), per_seq(3 * d)] + [_const_spec(c.shape) for c in consts],
        out_specs=[tok] * 8 + [per_seq(d)],
        out_shape=[jax.ShapeDtypeStruct((B, T, d), F32)] * 8 + [jax.ShapeDtypeStruct((B, 1, d), F32)],
        compiler_params=_params(("parallel", "arbitrary")),
        name="rwkv_pre",
    )(x, x, x_prev.reshape(B, 1, d), mod, *consts)
    return out


def _unit_lower_inverse(m_strict, c):
    n = m_strict.shape[-1]
    ti = _iota((n, n), 0)
    si = _iota((n, n), 1)
    eye = jnp.where(ti == si, 1.0, 0.0)
    sub = min(SUB, c)
    d_part = jnp.where(ti // sub == si // sub, m_strict, 0.0)
    t_d = eye + d_part
    pw = d_part
    for _ in range(int(math.log2(sub)) - 1):
        pw = _mm3(pw, pw, _BNN)
        t_d = t_d + _mm3(t_d, pw, _BNN)
    if sub == c:
        return t_d
    assert c // sub == 4
    nn = _mm3(t_d, m_strict - d_part, _BNN)
    n2 = _mm3(nn, nn, _BNN)
    return _mm3(eye + nn + n2 + _mm3(nn, n2, _BNN), t_d, _BNN)


def _rwkv_chunk_kernel(r_ref, ld_ref, k_ref, v_ref, kk_ref, a_ref, s0_ref, o_ref, s_out_ref, s_scr, *, c):
    ns, tb, d = r_ref.shape
    n = ns * c
    pairs = d // LANES
    half = LANES // 2
    j = pl.program_id(1)

    @pl.when(j == 0)
    def _():
        z = jnp.zeros((half, half), F32)
        for s in range(ns):
            for p in range(pairs):
                top = jnp.concatenate([s0_ref[s, 2 * p], z], axis=1)
                bot = jnp.concatenate([z, s0_ref[s, 2 * p + 1]], axis=1)
                s_scr[s, p] = jnp.concatenate([top, bot], axis=0)

    lane_a = _iota((n, LANES), 1) < half
    ti = _iota((2 * n, 2 * n), 0)
    si = _iota((2 * n, 2 * n), 1)
    same = ti // c == si // c
    strict = same & (si < ti)
    incl = same & (si <= ti)
    tc = _iota((n, n), 0)
    sc = _iota((n, n), 1)
    tri = jnp.where((tc // c == sc // c) & (sc <= tc), 1.0, 0.0).astype(BF16)
    last = jnp.where((tc // c == sc // c) & (sc % c == c - 1), 1.0, 0.0).astype(BF16)
    bd = (_iota((LANES, LANES), 0) < half) == (_iota((LANES, LANES), 1) < half)

    def load(ref):
        x = ref[...]
        if tb < c:
            x = jnp.concatenate([x, jnp.zeros((ns, c - tb, d), F32)], axis=1)
        return x.reshape(n, d)

    def by_pair(x):
        return jnp.stack([x[:, p * LANES:(p + 1) * LANES] for p in range(pairs)], axis=0)

    def stack(x):
        return jnp.concatenate([jnp.where(lane_a, x, 0.0), jnp.where(lane_a, 0.0, x)], axis=1)

    def unstack(x):
        return x[:, 0:n] + x[:, n:2 * n]

    r, ld, k, v, kk, a = (load(ref) for ref in (r_ref, ld_ref, k_ref, v_ref, kk_ref, a_ref))
    cum = _sel_mm3(tri, ld)
    cum_last = _sel_mm3(last, cum)
    g_end = jnp.exp(cum_last - cum)
    g_inv = jnp.exp(-cum)
    beta = kk * a
    a_t = by_pair(-kk * jnp.exp(cum - ld))
    r_t = by_pair(r * jnp.exp(cum))
    k_t = by_pair(k * g_inv)
    b_t = by_pair(beta * g_inv)
    at_st = stack(a_t)
    rt_st = stack(r_t)
    k2 = jnp.concatenate([k_t, k_t], axis=1)
    b2 = jnp.concatenate([b_t, b_t], axis=1)
    m_k = jnp.where(strict, _mm1(at_st, k2, _BNT), 0.0)
    m_b = jnp.where(strict, _mm1(at_st, b2, _BNT), 0.0)
    a_k = jnp.where(incl, _mm1(rt_st, k2, _BNT), 0.0)
    a_b = jnp.where(incl, _mm1(rt_st, b2, _BNT), 0.0)
    t_inv = _unit_lower_inverse(m_b, c)
    v_p = by_pair(v)
    v_st = stack(v_p)
    states = [s_scr[s] for s in range(ns)]
    w1 = jnp.concatenate([_mm1(a_t[:, s * c:(s + 1) * c], states[s], _BNT) for s in range(ns)], axis=1)
    o1 = jnp.concatenate([_mm1(r_t[:, s * c:(s + 1) * c], states[s], _BNT) for s in range(ns)], axis=1)
    sa_st = _mm1(t_inv, stack(w1) + _mm1(m_k, v_st, _BNN), _BNN)
    o = o1 + unstack(_mm1(a_k, v_st, _BNN) + _mm1(a_b, sa_st, _BNN))
    sa = unstack(sa_st)
    for p in range(pairs):
        o_ref[:, :, p * LANES:(p + 1) * LANES] = o[p].reshape(ns, c, LANES)[:, 0:tb, :]
    kg = by_pair(k * g_end)
    bg = by_pair(beta * g_end)
    for s in range(ns):
        rows = slice(s * c, (s + 1) * c)
        upd = _mm1(v_p[:, rows], kg[:, rows], _BTN) + _mm1(sa[:, rows], bg[:, rows], _BTN)
        decay = by_pair(jnp.exp(cum_last[s * c:s * c + 1, :]))
        s_scr[s] = states[s] * decay + jnp.where(bd, upd, 0.0)

    @pl.when(j == pl.num_programs(1) - 1)
    def _():
        for s in range(ns):
            for p in range(pairs):
                blk = s_scr[s, p]
                s_out_ref[s, 2 * p] = blk[0:half, 0:half]
                s_out_ref[s, 2 * p + 1] = blk[half:LANES, half:LANES]


def _rwkv_chunk(r, ld, k, v, kk, a, s0):
    B, T, d = r.shape
    heads = d // RW_HEAD
    if T >= 64:
        c, ns, tb = 64, 1, 64
    else:
        c, ns, tb = 16, 4, T
    assert T % tb == 0 and B % ns == 0 and tb <= c
    tok = pl.BlockSpec((ns, tb, d), lambda i, j: (i, j, 0))
    st = pl.BlockSpec((ns, heads, RW_HEAD, RW_HEAD), lambda i, j: (i, 0, 0, 0))
    return pl.pallas_call(
        functools.partial(_rwkv_chunk_kernel, c=c),
        grid=(B // ns, T // tb),
        in_specs=[tok] * 6 + [st],
        out_specs=[tok, st],
        out_shape=[jax.ShapeDtypeStruct((B, T, d), F32),
                   jax.ShapeDtypeStruct((B, heads, RW_HEAD, RW_HEAD), F32)],
        scratch_shapes=[pltpu.VMEM((ns, d // LANES, LANES, LANES), F32)],
        compiler_params=_params(("parallel", "arbitrary")),
        name="rwkv_chunk",
    )(r, ld, k, v, kk, a, s0)


def _residual_out(y, w_ref, x_ref, mod_ref, fg_ref, o_ref, final):
    bb, tt, d = x_ref.shape
    out = _mm1(y, w_ref[...]).reshape(bb, tt, d)
    x = x_ref[...] + mod_ref[:, :, 2 * d:3 * d] * out
    if final:
        x = x * lax.rsqrt(jnp.mean(x * x, axis=-1, keepdims=True) + NORM_EPS) * fg_ref[...]
    o_ref[...] = x


def _rwkv_post_kernel(o_ref, bonus_ref, g_ref, x_ref, mod_ref, gnw_ref, gnb_ref, wo_ref, fg_ref, out_ref, *, final):
    bb, tt, d = x_ref.shape
    m = bb * tt
    o = o_ref[...].reshape(m, d)
    mu = _head_bcast(_head_sum(o, RW_HEAD) * (1.0 / RW_HEAD), d, RW_HEAD)
    oc = o - mu
    var = _head_bcast(_head_sum(oc * oc, RW_HEAD) * (1.0 / RW_HEAD), d, RW_HEAD)
    y = oc * lax.rsqrt(var + RW_GN_EPS) * gnw_ref[...] + gnb_ref[...]
    y = (y + bonus_ref[...].reshape(m, d)) * g_ref[...].reshape(m, d)
    _residual_out(y, wo_ref, x_ref, mod_ref, fg_ref, out_ref, final)


def _post_call(kernel, name, x, mod, tok_inputs, consts, final):
    B, T, d = x.shape
    bb, tt = _row_tiles(B, T)
    tok = lambda a: pl.BlockSpec((bb, tt, a.shape[-1]), lambda i, j: (i, j, 0))
    return pl.pallas_call(
        functools.partial(kernel, final=final),
        grid=(B // bb, T // tt),
        in_specs=[tok(a) for a in tok_inputs] + [tok(x), pl.BlockSpec((bb, 1, 3 * d), lambda i, j: (i, 0, 0))]
        + [_const_spec(c.shape) for c in consts],
        out_specs=tok(x),
        out_shape=jax.ShapeDtypeStruct((B, T, d), F32),
        compiler_params=_params(("parallel", "parallel")),
        name=name,
    )(*tok_inputs, x, mod, *consts)


def _rwkv_layer(x, mod, norm_g, final_g, final, x_prev, s0, p):
    d = x.shape[-1]
    r, ld, k, v, kk, a, g, bonus, shift = _rwkv_pre(x, x_prev, mod, norm_g, p)
    o, s_new = _rwkv_chunk(r, ld, k, v, kk, a, s0)
    consts = [p['gn_w'].reshape(1, d), p['gn_b'].reshape(1, d), p['w_o'].astype(BF16), final_g.reshape(1, d)]
    x = _post_call(_rwkv_post_kernel, "rwkv_post", x, mod, [o, bonus, g], consts, final)
    return x, s_new, shift[:, 0, :]


def _gla_pre_kernel(x_ref, mod_ref, ng_ref, w_ref, wl_ref, w2_ref, b2_ref, q_out, k_out, v_out, gate_out, gk_out,
                    *, qk):
    bb, tt, d = x_ref.shape
    m = bb * tt
    h = _prenorm(x_ref[...], ng_ref[...], mod_ref[...], d).reshape(m, d).astype(BF16)
    dk = qk // GLA_HEADS
    q_out[...] = (_dot(h, w_ref[:, 0:qk]) * dk ** -0.5).reshape(bb, tt, qk)
    k_out[...] = _dot(h, w_ref[:, qk:2 * qk]).reshape(bb, tt, qk)
    v_out[...] = _dot(h, w_ref[:, 2 * qk:2 * qk + d]).reshape(bb, tt, d)
    gate_out[...] = _dot(h, w_ref[:, 2 * qk + d:2 * qk + 2 * d]).reshape(bb, tt, d)
    low = _dot(h, wl_ref[...])
    gk = _mm3(low, w2_ref[...]) + b2_ref[...]
    gk_out[...] = (-_softplus(-gk) * (1.0 / GLA_GATE_NORMALIZER)).reshape(bb, tt, qk)


def _hg_pre_kernel(x_ref, mod_ref, ng_ref, w_ref, lower_ref, q_out, k_out, v_out, gate_out, g_out, *, layer):
    bb, tt, d = x_ref.shape
    m = bb * tt
    h = _prenorm(x_ref[...], ng_ref[...], mod_ref[...], d).reshape(m, d).astype(BF16)
    dk = d // HG_HEADS
    low = lower_ref[...]
    e = jnp.exp(low - jnp.max(low, axis=0, keepdims=True))
    soft = e / jnp.sum(e, axis=0, keepdims=True)
    lb = jnp.sum(soft[0:layer + 1], axis=0, keepdims=True) - soft[0:1]
    q = _dot(h, w_ref[:, 0:d])
    f = _dot(h, w_ref[:, d:2 * d])
    forget = lb + (1.0 - lb) * _sigmoid(f)
    q_out[...] = (_silu(q) * dk ** -0.5).reshape(bb, tt, d)
    k_out[...] = (1.0 - forget).reshape(bb, tt, d)
    g_out[...] = jnp.log(forget).reshape(bb, tt, d)
    v_out[...] = _dot(h, w_ref[:, 2 * d:3 * d]).reshape(bb, tt, d)
    gate_out[...] = _dot(h, w_ref[:, 3 * d:4 * d]).reshape(bb, tt, d)


def _pre_call(kernel, name, x, mod, consts, out_widths):
    B, T, d = x.shape
    bb, tt = _row_tiles(B, T)
    tok = lambda n: pl.BlockSpec((bb, tt, n), lambda i, j: (i, j, 0))
    return pl.pallas_call(
        kernel,
        grid=(B // bb, T // tt),
        in_specs=[tok(d), pl.BlockSpec((bb, 1, 3 * d), lambda i, j: (i, 0, 0))]
        + [_const_spec(c.shape) for c in consts],
        out_specs=[tok(n) for n in out_widths],
        out_shape=[jax.ShapeDtypeStruct((B, T, n), F32) for n in out_widths],
        compiler_params=_params(("parallel", "parallel")),
        name=name,
    )(x, mod, *consts)


def _gla_chunk_kernel(q_ref, k_ref, g_ref, v_ref, s0_ref, o_ref, s_out_ref, st_scr, *, c, heads):
    tb = q_ref.shape[1]
    dk = q_ref.shape[2] // heads
    dv = v_ref.shape[2] // heads
    j = pl.program_id(1)

    @pl.when(j == 0)
    def _():
        for h in range(heads):
            st_scr[h] = s0_ref[0, h].T

    def load(ref, w):
        x = ref[0]
        if tb < c:
            x = jnp.concatenate([x, jnp.zeros((c - tb, x.shape[1]), F32)], axis=0)
        return jnp.stack([x[:, h * w:(h + 1) * w] for h in range(heads)], axis=0)

    q, k, g, v = load(q_ref, dk), load(k_ref, dk), load(g_ref, dk), load(v_ref, dv)
    tc = _iota((c, c), 0)
    sc = _iota((c, c), 1)
    tri = jnp.where(sc <= tc, 1.0, 0.0).astype(BF16)
    b = jnp.stack([_sel_mm3(tri, g[h]) for h in range(heads)], axis=0)
    st = st_scr[...]
    o_inter = _mm3(q * jnp.exp(b), st, _BNT)
    sub = min(GLA_SUB, c)
    key_row = _iota((c, 1), 0)
    row = _iota((sub, 1), 0)
    outs = []
    for i in range(c // sub):
        r0 = i * sub
        qi = q[:, r0:r0 + sub]
        bi = b[:, r0:r0 + sub]
        acc = o_inter[:, r0:r0 + sub]
        if i > 0:
            ref_b = b[:, r0 - 1:r0]
            k_fac = jnp.exp(jnp.where(key_row < r0, ref_b - b, -jnp.inf))
            att = _mm3(qi * jnp.exp(bi - ref_b), k * k_fac, _BNT)
            acc = acc + _mm3(att, v, _BNN)
        for s in range(sub):
            diff = jnp.where(row >= s, bi - bi[:, s:s + 1], -jnp.inf)
            col = jnp.sum(qi * k[:, r0 + s:r0 + s + 1] * jnp.exp(diff), axis=2, keepdims=True)
            acc = acc + col * v[:, r0 + s:r0 + s + 1]
        outs.append(acc)
    o = jnp.concatenate(outs, axis=1) if len(outs) > 1 else outs[0]
    for h in range(heads):
        o_ref[0, :, h * dv:(h + 1) * dv] = o[h, 0:tb]
    b_last = b[:, c - 1:c]
    st_new = st * jnp.exp(b_last) + _mm3(v, k * jnp.exp(b_last - b), _BTN)
    st_scr[...] = st_new

    @pl.when(j == pl.num_programs(1) - 1)
    def _():
        for h in range(heads):
            s_out_ref[0, h] = st_new[h].T


def _gla_chunk(q, k, g, v, s0, heads):
    B, T, qk = q.shape
    d = v.shape[-1]
    dk = qk // heads
    dv = d // heads
    tb = 64 if T % 64 == 0 else T
    c = -(-tb // SUB) * SUB
    kspec = pl.BlockSpec((1, tb, qk), lambda b, j: (b, j, 0))
    vspec = pl.BlockSpec((1, tb, d), lambda b, j: (b, j, 0))
    sspec = pl.BlockSpec((1, heads, dk, dv), lambda b, j: (b, 0, 0, 0))
    return pl.pallas_call(
        functools.partial(_gla_chunk_kernel, c=c, heads=heads),
        grid=(B, T // tb),
        in_specs=[kspec, kspec, kspec, vspec, sspec],
        out_specs=[vspec, sspec],
        out_shape=[jax.ShapeDtypeStruct((B, T, d), F32),
                   jax.ShapeDtypeStruct((B, heads, dk, dv), F32)],
        scratch_shapes=[pltpu.VMEM((heads, dv, dk), F32)],
        compiler_params=_params(("parallel", "arbitrary")),
        name="gla_chunk",
    )(q, k, g, v, s0)


def _gla_post_kernel(o_ref, gate_ref, x_ref, mod_ref, gnw_ref, wo_ref, fg_ref, out_ref, *, final, heads):
    bb, tt, d = x_ref.shape
    m = bb * tt
    hs = d // heads
    o = o_ref[...].reshape(m, d)
    ms = _head_bcast(_head_sum(o * o, hs) * (1.0 / hs), d, hs)
    y = o * lax.rsqrt(ms + NORM_EPS) * gnw_ref[...]
    y = y * _silu(gate_ref[...].reshape(m, d))
    _residual_out(y, wo_ref, x_ref, mod_ref, fg_ref, out_ref, final)


def _gla_layer(x, mod, norm_g, final_g, final, s0, p):
    d = x.shape[-1]
    qk = p['w_gk2'].shape[1]
    w_in = p['w_in']
    consts = [norm_g.reshape(1, d), w_in[:, :2 * qk + 2 * d].astype(BF16),
              _pad_cols(w_in[:, 2 * qk + 2 * d:], LANES).astype(BF16), _pad_rows(p['w_gk2'], LANES),
              p['b_gk2'].reshape(1, qk)]
    q, k, v, gate, gk = _pre_call(functools.partial(_gla_pre_kernel, qk=qk), "gla_pre", x, mod, consts,
                                  (qk, qk, d, d, qk))
    o, s_new = _gla_chunk(q, k, gk, v, s0, GLA_HEADS)
    post_consts = [jnp.tile(p['gn_w'], GLA_HEADS).reshape(1, d), p['w_o'].astype(BF16), final_g.reshape(1, d)]
    x = _post_call(functools.partial(_gla_post_kernel, heads=GLA_HEADS), "gla_post", x, mod, [o, gate], post_consts,
                   final)
    return x, s_new


def _hg_layer(x, mod, norm_g, final_g, final, s0, layer, p):
    d = x.shape[-1]
    consts = [norm_g.reshape(1, d), p['w_in'].astype(BF16), p['lower']]
    q, k, v, gate, g = _pre_call(functools.partial(_hg_pre_kernel, layer=layer), "hgrn_pre", x, mod, consts,
                                 (d, d, d, d, d))
    o, s_new = _gla_chunk(q, k, g, v, s0, HG_HEADS)
    post_consts = [jnp.tile(p['gn_w'], HG_HEADS).reshape(1, d), p['w_o'].astype(BF16), final_g.reshape(1, d)]
    x = _post_call(functools.partial(_gla_post_kernel, heads=HG_HEADS), "hgrn_post", x, mod, [o, gate], post_consts,
                   final)
    return x, s_new


def _sb_pre_kernel(x_ref, mod_ref, ng_ref, w_ref, q_out, k_out, v_out, gate_out):
    bb, tt, d = x_ref.shape
    m = bb * tt
    h = _prenorm(x_ref[...], ng_ref[...], mod_ref[...], d).reshape(m, d).astype(BF16)
    hd = d // SB_HEADS
    q_out[...] = (_dot(h, w_ref[:, 0:d]) * hd ** -0.5).reshape(bb, tt, d)
    k_out[...] = _dot(h, w_ref[:, d:2 * d]).reshape(bb, tt, d)
    v_out[...] = _dot(h, w_ref[:, 2 * d:3 * d]).reshape(bb, tt, d)
    gate_out[...] = _dot(h, w_ref[:, 3 * d:4 * d]).reshape(bb, tt, d)


def _sb_prompt_kernel(bias_ref, q_ref, k_ref, v_ref, o_ref, acc_scr, z_scr, sp_scr, *, tq):
    p = pl.program_id(1)
    i = pl.program_id(2)
    half = LANES // 2
    q = q_ref[0] * LOG2E
    lane_a = _iota((tq, LANES), 1) < half
    q_st = jnp.concatenate([jnp.where(lane_a, q, 0.0), jnp.where(lane_a, 0.0, q)], axis=0).astype(BF16)
    bias = jnp.where(_iota((2 * tq, 1), 0) < tq, bias_ref[2 * p], bias_ref[2 * p + 1]) * LOG2E
    tri = jnp.where(_iota((tq, tq), 0) >= _iota((tq, tq), 1), 1.0, 0.0).astype(BF16)

    def block_rows(jb):
        return pl.ds(pl.multiple_of(jnp.maximum(jb, 0) * tq, tq), tq)

    def scores(jb):
        z2 = _dot(q_st, k_ref[0, block_rows(jb), :].astype(BF16), _NT) + bias
        sp2 = jnp.maximum(z2, 0.0) + jnp.log2(1.0 + jnp.exp2(-jnp.abs(z2)))
        return z2, sp2

    def score_into(slot, jb):
        z2, sp2 = scores(jb)
        z_scr[slot] = z2
        sp_scr[slot] = sp2.astype(BF16)

    def finish(slot, jb, carry):
        cs = _dot(sp_scr[slot], tri)
        w = jnp.exp2(z_scr[slot] - cs - carry)
        acc_scr[...] += _dot(w.astype(BF16), v_ref[0, block_rows(jb), :].astype(BF16))
        return carry + cs[:, 0:1]

    score_into(0, i - 1)

    z2, sp2 = scores(i)
    before = _iota((2 * tq, tq), 1) < _iota((2 * tq, tq), 0) % tq
    spm = jnp.where(before, sp2, 0.0).astype(BF16)
    cs = _dot(spm, tri)
    w = jnp.where(before, jnp.exp2((z2 - sp2) - (cs - spm.astype(F32))), 0.0)
    acc_scr[...] = _dot(w.astype(BF16), v_ref[0, block_rows(i), :].astype(BF16))
    carry = cs[:, 0:1]

    def pair(n, carry):
        jb = i - 1 - 2 * n
        score_into(1, jb - 1)
        carry = finish(0, jb, carry)
        score_into(0, jb - 2)
        return finish(1, jb - 1, carry)

    carry = lax.fori_loop(0, i // 2, pair, carry)

    @pl.when(i % 2 == 1)
    def _():
        finish(0, 0, carry)

    acc = acc_scr[...]
    o_ref[0] = jnp.where(lane_a, acc[0:tq], acc[tq:2 * tq])


def _sb_prompt(q, k, v, bias):
    B, T, d = q.shape
    tq = min(T, 256)
    qspec = pl.BlockSpec((1, tq, LANES), lambda b, p, i: (b, i, p))
    kspec = pl.BlockSpec((1, T, LANES), lambda b, p, i: (b, 0, p))
    return pl.pallas_call(
        functools.partial(_sb_prompt_kernel, tq=tq),
        grid=(B, d // LANES, T // tq),
        in_specs=[pl.BlockSpec(memory_space=pltpu.SMEM), qspec, kspec, kspec],
        out_specs=qspec,
        out_shape=jax.ShapeDtypeStruct((B, T, d), F32),
        scratch_shapes=[pltpu.VMEM((2 * tq, LANES), F32), pltpu.VMEM((2, 2 * tq, tq), F32),
                        pltpu.VMEM((2, 2 * tq, tq), BF16)],
        compiler_params=_params(("parallel", "parallel", "arbitrary")),
        name="sb_attention_prompt",
    )(bias, q, k, v)


def _sb_paged_kernel(pt_ref, q_ref, kn_ref, vn_ref, bias_ref, tri_ref, *refs, n_blocks):
    kc_refs, vc_refs = refs[:n_blocks], refs[n_blocks:2 * n_blocks]
    o_ref, acc_scr, carry_scr = refs[2 * n_blocks:]
    t, d = q_ref.shape[1], q_ref.shape[2]
    heads = LANES // t
    hd = d // heads
    j = pl.program_id(1)
    q = q_ref[0] * LOG2E
    head_of_row = _iota((LANES, d), 0) // t
    head_of_lane = _iota((LANES, d), 1) // hd
    q_bd = jnp.where(head_of_lane == head_of_row, jnp.concatenate([q] * heads, axis=0), 0.0).astype(BF16)
    bias = bias_ref[...] * LOG2E

    def attend(k, v, nk, before, carry, qk_dims, pv_dims):
        z2 = _dot(q_bd, k, qk_dims) + bias[:, 0:nk]
        sp2 = jnp.maximum(z2, 0.0) + jnp.log2(1.0 + jnp.exp2(-jnp.abs(z2)))
        spb = (sp2 if before is None else jnp.where(before, sp2, 0.0)).astype(BF16)
        cs = _dot(spb, tri_ref[0:nk, 0:nk])
        if before is None:
            w = jnp.exp2(z2 - cs - jnp.concatenate([carry] * (nk // LANES), axis=1))
        else:
            w = jnp.where(before, jnp.exp2((z2 - sp2) - (cs - spb.astype(F32))), 0.0)
        total = _dot(spb, jnp.ones((nk, LANES), BF16))
        return _dot(w.astype(BF16), v, pv_dims), total

    @pl.when(j == 0)
    def _():
        nk = 2 * t
        pad = jnp.zeros((nk - t, d), F32)
        before = _iota((LANES, nk), 1) < _iota((LANES, nk), 0) % t
        upd, total = attend(jnp.concatenate([kn_ref[0], pad], axis=0).astype(BF16),
                            jnp.concatenate([vn_ref[0], pad], axis=0).astype(BF16), nk, before, None, _NT, _NN)
        acc_scr[...] = upd
        carry_scr[...] = total

    page = kc_refs[0].shape[2]

    def joined(page_refs):
        return jnp.concatenate([r[...].reshape(d, page) for r in reversed(page_refs)], axis=1).astype(BF16)

    upd, total = attend(joined(kc_refs), joined(vc_refs), n_blocks * page, None, carry_scr[...], _NN, _NT)
    acc_scr[...] += upd
    carry_scr[...] += total

    @pl.when(j == pl.num_programs(1) - 1)
    def _():
        full = acc_scr[...]
        out = jnp.zeros((t, d), F32)
        for h in range(heads):
            out = jnp.where(head_of_lane[0:t] == h, full[h * t:(h + 1) * t, :], out)
        o_ref[0] = out


def _sb_paged(q, k_new, v_new, bias, k_cache, v_cache, layer, page_table):
    B, T, d = q.shape
    _, _, page, heads, hd = k_cache.shape
    n_pages = page_table.shape[1]
    n_blocks = 4 if n_pages % 4 == 0 else 1
    assert heads * T == LANES and heads * hd == d
    tok = pl.BlockSpec((1, T, d), lambda b, j, pt: (b, 0, 0))

    def cache(i):
        return pl.BlockSpec((None, None, heads, hd, page),
                            lambda b, j, pt: (layer, pt[b, n_pages - 1 - (j * n_blocks + i)], 0, 0, 0))

    nk = n_blocks * page
    bias_rows = jnp.broadcast_to(jnp.repeat(bias, T)[:, None], (LANES, nk))
    tri = (jnp.arange(nk)[:, None] >= jnp.arange(nk)[None, :]).astype(BF16)
    grid_spec = pltpu.PrefetchScalarGridSpec(
        num_scalar_prefetch=1,
        grid=(B, n_pages // n_blocks),
        in_specs=[tok, tok, tok, pl.BlockSpec((LANES, nk), lambda b, j, pt: (0, 0)),
                  pl.BlockSpec((nk, nk), lambda b, j, pt: (0, 0))]
        + [cache(i) for i in range(n_blocks)] * 2,
        out_specs=tok,
        scratch_shapes=[pltpu.VMEM((LANES, d), F32), pltpu.VMEM((LANES, LANES), F32)],
    )
    return pl.pallas_call(
        functools.partial(_sb_paged_kernel, n_blocks=n_blocks),
        grid_spec=grid_spec,
        out_shape=jax.ShapeDtypeStruct((B, T, d), F32),
        compiler_params=_params(("parallel", "arbitrary")),
        name="sb_attention_paged",
    )(page_table, q, k_new, v_new, bias_rows, tri, *([jnp.transpose(k_cache, (0, 1, 3, 4, 2))] * n_blocks),
      *([jnp.transpose(v_cache, (0, 1, 3, 4, 2))] * n_blocks))


def _sb_post_kernel(o_ref, gate_ref, x_ref, mod_ref, wo_ref, fg_ref, out_ref, *, final):
    bb, tt, d = x_ref.shape
    m = bb * tt
    y = o_ref[...].reshape(m, d) * _silu(gate_ref[...].reshape(m, d))
    _residual_out(y, wo_ref, x_ref, mod_ref, fg_ref, out_ref, final)


def _sb_layer(x, mod, norm_g, final_g, final, past, p):
    B, T, d = x.shape
    consts = [norm_g.reshape(1, d), p['w_in'].astype(BF16)]
    q, k, v, gate = _pre_call(_sb_pre_kernel, "sb_pre", x, mod, consts, (d, d, d, d))
    if past is None:
        o = _sb_prompt(q, k, v, p['bias'])
    else:
        o = _sb_paged(q, k, v, p['bias'], *past)
    x = _post_call(_sb_post_kernel, "sb_post", x, mod, [o, gate], [p['w_o'].astype(BF16), final_g.reshape(1, d)],
                   final)
    hd = d // SB_HEADS
    return x, k.reshape(B, T, SB_HEADS, hd), v.reshape(B, T, SB_HEADS, hd)


def kernel(x_prompt, x_sample, c_prompt, c_sample, state_rwkv, cache_rwkv_shift, state_gla, cache_sb_k, cache_sb_v, state_hgrn, page_table, norm_g, ada_w, ada_b, final_g, rw_mix, rw_w_rkvg, rw_w0, rw_w1, rw_w2, rw_a0, rw_a1, rw_a2, rw_k_k, rw_k_a, rw_r_k, rw_gn_w, rw_gn_b, rw_w_o, gla_w_in, gla_w_gk2, gla_b_gk2, gla_gn_w, gla_w_o, sb_w_in, sb_bias, sb_w_o, hg_w_in, hg_lower, hg_gn_w, hg_w_o):
    depth, d = norm_g.shape
    n_mix = 4
    bp, bs = x_prompt.shape[0], x_sample.shape[0]
    rows = -(-(bp + bs) // 8) * 8
    c_all = jnp.pad(jnp.concatenate([c_prompt, c_sample], axis=0), ((0, rows - bp - bs), (0, 0)))
    mods = _modulation(c_all, ada_w, ada_b)

    def trunk(x, mod_all, rw_s, rw_shift, gla_s, sb_past, hg_s):
        B = x.shape[0]
        outs = {n: [] for n in ('rw_s', 'rw_shift', 'gla_s', 'sb_k', 'sb_v', 'hg_s')}
        for i in range(depth):
            kind, j = i % n_mix, i // n_mix
            mod = mod_all[i].reshape(B, 1, 3 * d)
            final = i == depth - 1
            if kind == 0:
                p = dict(mix=rw_mix[j], w_rkvg=rw_w_rkvg[j], w0=rw_w0[j], w1=rw_w1[j], w2=rw_w2[j], a0=rw_a0[j],
                         a1=rw_a1[j], a2=rw_a2[j], k_k=rw_k_k[j], k_a=rw_k_a[j], r_k=rw_r_k[j], gn_w=rw_gn_w[j],
                         gn_b=rw_gn_b[j], w_o=rw_w_o[j])
                x, s, sh = _rwkv_layer(x, mod, norm_g[i], final_g, final, rw_shift[j], rw_s[j], p)
                outs['rw_s'].append(s)
                outs['rw_shift'].append(sh)
            elif kind == 1:
                p = dict(w_in=gla_w_in[j], w_gk2=gla_w_gk2[j], b_gk2=gla_b_gk2[j], gn_w=gla_gn_w[j], w_o=gla_w_o[j])
                x, s = _gla_layer(x, mod, norm_g[i], final_g, final, gla_s[j], p)
                outs['gla_s'].append(s)
            elif kind == 2:
                p = dict(w_in=sb_w_in[j], bias=sb_bias[j], w_o=sb_w_o[j])
                x, k_new, v_new = _sb_layer(x, mod, norm_g[i], final_g, final, sb_past(j), p)
                outs['sb_k'].append(k_new)
                outs['sb_v'].append(v_new)
            else:
                p = dict(w_in=hg_w_in[j], lower=hg_lower, gn_w=hg_gn_w[j], w_o=hg_w_o[j])
                x, s = _hg_layer(x, mod, norm_g[i], final_g, final, hg_s[j], i, p)
                outs['hg_s'].append(s)
        st = lambda n: jnp.stack(outs[n])
        return x, st('rw_s'), st('rw_shift'), st('gla_s'), st('sb_k'), st('sb_v'), st('hg_s')

    z = lambda a, b: jnp.zeros((a.shape[0], b) + a.shape[2:], a.dtype)
    prompt = trunk(x_prompt, mods[:, :bp], z(state_rwkv, bp), z(cache_rwkv_shift, bp), z(state_gla, bp),
                   lambda j: None, z(state_hgrn, bp))
    sample = trunk(x_sample, mods[:, bp:bp + bs], state_rwkv, cache_rwkv_shift, state_gla,
                   lambda j: (cache_sb_k, cache_sb_v, j, page_table), state_hgrn)
    return (prompt[0], sample[0]) + prompt[1:] + sample[1:]
```

```python
import functools
import math

import jax
import jax.numpy as jnp
from jax import lax
from jax.experimental import pallas as pl
from jax.experimental.pallas import tpu as pltpu

F32 = jnp.float32
BF16 = jnp.bfloat16

NORM_EPS = 1e-6
RW_HEAD = 64
RW_GN_EPS = 64e-5
GLA_HEADS = 4
GLA_GATE_NORMALIZER = 16.0
SB_HEADS = 16
HG_HEADS = 8
LANES = 128
ROW_TILE = 512
SUB = 16
GLA_SUB = 16
VMEM_LIMIT = 56 * 1024 * 1024
LOG2E = 1.4426950408889634

_NN = (((1,), (0,)), ((), ()))
_NT = (((1,), (1,)), ((), ()))
_TN = (((0,), (0,)), ((), ()))
_BNN = (((2,), (1,)), ((0,), (0,)))
_BNT = (((2,), (2,)), ((0,), (0,)))
_BTN = (((1,), (1,)), ((0,), (0,)))


def _dot(a, b, dims=_NN):
    return lax.dot_general(a, b, dims, preferred_element_type=F32)


def _split(x):
    hi = x.astype(BF16)
    lo = (x - hi.astype(F32)).astype(BF16)
    return hi, lo


def _mm1(a, b, dims=_NN):
    return _dot(a.astype(BF16), b.astype(BF16), dims)


def _mm3(a, b, dims=_NN):
    ah, al = _split(a)
    bh, bl = _split(b)
    return _dot(ah, bh, dims) + (_dot(ah, bl, dims) + _dot(al, bh, dims))


def _mm_sel(a, sel, dims=_NN):
    ah, al = _split(a)
    return _dot(ah, sel, dims) + _dot(al, sel, dims)


def _mm_sel_fused(a, sel2):
    ah, al = _split(a)
    return _dot(jnp.concatenate([ah, al], axis=1), sel2)


def _sel_mm(sel, b, dims=_NN):
    bh, bl = _split(b)
    return _dot(sel, bh, dims) + _dot(sel, bl, dims)


def _sel_mm3(sel, b):
    b1 = b.astype(BF16)
    r1 = b - b1.astype(F32)
    b2 = r1.astype(BF16)
    b3 = (r1 - b2.astype(F32)).astype(BF16)
    return _dot(sel, b1) + (_dot(sel, b2) + _dot(sel, b3))


def _silu(x):
    return x * (1.0 / (1.0 + jnp.exp(-x)))


def _sigmoid(x):
    return 1.0 / (1.0 + jnp.exp(-x))


def _softplus(x):
    return jnp.maximum(x, 0.0) + jnp.log1p(jnp.exp(-jnp.abs(x)))


def _iota(shape, dim):
    return lax.broadcasted_iota(jnp.int32, shape, dim)


def _head_indicator(d, hs):
    return jnp.where(_iota((d, LANES), 0) // hs == _iota((d, LANES), 1), 1.0, 0.0).astype(BF16)


def _head_indicator_t(d, hs):
    return jnp.where(_iota((LANES, d), 1) // hs == _iota((LANES, d), 0), 1.0, 0.0).astype(BF16)


def _head_sum(x, hs):
    return _mm_sel(x, _head_indicator(x.shape[-1], hs))


def _head_bcast(s, d, hs):
    return _mm_sel(s, _head_indicator_t(d, hs))


def _row_tiles(B, T):
    tt = min(T, ROW_TILE)
    bb = max(1, min(B, ROW_TILE // tt))
    assert T % tt == 0 and B % bb == 0 and tt % 8 == 0
    return bb, tt


def _params(sem):
    return pltpu.CompilerParams(dimension_semantics=sem, vmem_limit_bytes=VMEM_LIMIT)


def _const_spec(shape):
    n = len(shape)
    return pl.BlockSpec(shape, lambda *_: (0,) * n)


def _prenorm(x, g, mod, d):
    shift = mod[:, :, 0:d]
    scale = mod[:, :, d:2 * d]
    y = x * lax.rsqrt(jnp.mean(x * x, axis=-1, keepdims=True) + NORM_EPS)
    return (y * g) * (1.0 + scale) + shift


def _mod_kernel(c_ref, w_ref, b_ref, o_ref):
    o_ref[0] = _mm1(_silu(c_ref[...]), w_ref[0]) + b_ref[0]


def _modulation(c, ada_w, ada_b):
    depth, d, n = ada_w.shape
    rows = c.shape[0]
    tn = 1536
    return pl.pallas_call(
        _mod_kernel,
        grid=(depth, n // tn),
        in_specs=[pl.BlockSpec((rows, d), lambda l, j: (0, 0)),
                  pl.BlockSpec((1, d, tn), lambda l, j: (l, 0, j)),
                  pl.BlockSpec((1, 1, tn), lambda l, j: (l, 0, j))],
        out_specs=pl.BlockSpec((1, rows, tn), lambda l, j: (l, 0, j)),
        out_shape=jax.ShapeDtypeStruct((depth, rows, n), F32),
        compiler_params=_params(("parallel", "parallel")),
        name="ada_modulation",
    )(c, ada_w, ada_b.reshape(depth, 1, n))


def _rwkv_pre_kernel(x_ref, halo_ref, xprev_ref, mod_ref, ng_ref, mix_ref, wr_ref, wk_ref, wv_ref, wg_ref,
                     w0_ref, w1_ref, w2_ref, a0_ref, a1_ref, a2_ref, kk_ref, ka_ref, rk_ref,
                     r_out, ld_out, k_out, v_out, kkn_out, a_out, g_out, bonus_out, shift_out):
    bb, tt, d = x_ref.shape
    m = bb * tt
    mod = mod_ref[...]
    g = ng_ref[...]
    h3 = _prenorm(x_ref[...], g, mod, d)
    h_halo = _prenorm(halo_ref[:, 7:8, :], g, mod, d)
    h_prev = jnp.where(pl.program_id(1) == 0, xprev_ref[...], h_halo)
    shift_out[...] = h3[:, tt - 1:tt, :]
    h = h3.reshape(m, d)
    prev = jnp.where(_iota((m, 1), 0) % tt == 0,
                     jnp.broadcast_to(h_prev, (bb, tt, d)).reshape(m, d),
                     pltpu.roll(h, 1, axis=0))
    xx = prev - h
    xr, xw, xk, xv, xa, xg = (h + xx * mix_ref[n:n + 1, :] for n in range(6))
    r = _mm1(xr, wr_ref[...])
    k = _mm1(xk, wk_ref[...])
    v = _mm1(xv, wv_ref[...])
    gate = _silu(_mm1(xg, wg_ref[...]))
    lw = w0_ref[...] + _mm1(jnp.tanh(_mm1(xw, w1_ref[...])), w2_ref[...])
    log_w = -_softplus(-lw) - 0.5
    a = _sigmoid(a0_ref[...] + _mm1(_mm1(xa, a1_ref[...]), a2_ref[...]))
    kk = k * kk_ref[...]
    nrm = jnp.maximum(jnp.sqrt(_head_sum(kk * kk, RW_HEAD)), 1e-12)
    kk = kk * _head_bcast(1.0 / nrm, d, RW_HEAD)
    k = k * (1.0 + (a - 1.0) * ka_ref[...])
    bonus = _head_bcast(_head_sum(r * k * rk_ref[...], RW_HEAD), d, RW_HEAD) * v
    for ref, val in ((r_out, r), (ld_out, -jnp.exp(log_w)), (k_out, k), (v_out, v), (kkn_out, kk),
                     (a_out, a), (g_out, gate), (bonus_out, bonus)):
        ref[...] = val.reshape(bb, tt, d)


def _pad_cols(w, n):
    return jnp.pad(w, ((0, 0), (0, n - w.shape[1])))


def _pad_rows(w, n):
    return jnp.pad(w, ((0, n - w.shape[0]), (0, 0)))


def _rwkv_pre(x, x_prev, mod, norm_g, p):
    B, T, d = x.shape
    tt = min(T, ROW_TILE // 2)
    bb = max(1, min(B, (ROW_TILE // 2) // tt))
    row = lambda a: a.reshape(1, d)
    tok = pl.BlockSpec((bb, tt, d), lambda i, j: (i, j, 0))
    per_seq = lambda n: pl.BlockSpec((bb, 1, n), lambda i, j: (i, 0, 0))
    halo = pl.BlockSpec((bb, 8, d), lambda i, j: (i, jnp.maximum(j * (tt // 8) - 1, 0), 0))
    consts = [row(norm_g), p['mix'],
              p['w_rkvg'][0].astype(BF16), p['w_rkvg'][1].astype(BF16), p['w_rkvg'][2].astype(BF16),
              p['w_rkvg'][3].astype(BF16),
              row(p['w0']), _pad_cols(p['w1'], LANES).astype(BF16), _pad_rows(p['w2'], LANES).astype(BF16),
              row(p['a0']), _pad_cols(p['a1'], LANES).astype(BF16), _pad_rows(p['a2'], LANES).astype(BF16),
              row(p['k_k']), row(p['k_a']), row(p['r_k'])]
    out = pl.pallas_call(
        _rwkv_pre_kernel,
        grid=(B // bb, T // tt),
        in_specs=[tok, halo, per_seq(d), per_seq(3 * d)] + [_const_spec(c.shape) for c in consts],
        out_specs=[tok] * 8 + [per_seq(d)],
        out_shape=[jax.ShapeDtypeStruct((B, T, d), F32)] * 8 + [jax.ShapeDtypeStruct((B, 1, d), F32)],
        compiler_params=_params(("parallel", "arbitrary")),
        name="rwkv_pre",
    )(x, x, x_prev.reshape(B, 1, d), mod, *consts)
    return out


def _unit_lower_inverse(m_strict, c):
    n = m_strict.shape[-1]
    ti = _iota((n, n), 0)
    si = _iota((n, n), 1)
    eye = jnp.where(ti == si, 1.0, 0.0)
    sub = min(SUB, c)
    d_part = jnp.where(ti // sub == si // sub, m_strict, 0.0)
    t_d = eye + d_part
    pw = d_part
    for _ in range(int(math.log2(sub)) - 1):
        pw = _mm3(pw, pw, _BNN)
        t_d = t_d + _mm3(t_d, pw, _BNN)
    if sub == c:
        return t_d
    assert c // sub == 4
    nn = _mm3(t_d, m_strict - d_part, _BNN)
    n2 = _mm3(nn, nn, _BNN)
    return _mm3(eye + nn + n2 + _mm3(nn, n2, _BNN), t_d, _BNN)


def _rwkv_chunk_kernel(r_ref, ld_ref, k_ref, v_ref, kk_ref, a_ref, s0_ref, o_ref, s_out_ref, s_scr, *, c):
    ns, tb, d = r_ref.shape
    n = ns * c
    pairs = d // LANES
    half = LANES // 2
    j = pl.program_id(1)

    @pl.when(j == 0)
    def _():
        z = jnp.zeros((half, half), F32)
        for s in range(ns):
            for p in range(pairs):
                top = jnp.concatenate([s0_ref[s, 2 * p], z], axis=1)
                bot = jnp.concatenate([z, s0_ref[s, 2 * p + 1]], axis=1)
                s_scr[s, p] = jnp.concatenate([top, bot], axis=0)

    lane_a = _iota((n, LANES), 1) < half
    ti = _iota((2 * n, 2 * n), 0)
    si = _iota((2 * n, 2 * n), 1)
    same = ti // c == si // c
    strict = same & (si < ti)
    incl = same & (si <= ti)
    tc = _iota((n, n), 0)
    sc = _iota((n, n), 1)
    tri = jnp.where((tc // c == sc // c) & (sc <= tc), 1.0, 0.0).astype(BF16)
    last = jnp.where((tc // c == sc // c) & (sc % c == c - 1), 1.0, 0.0).astype(BF16)
    bd = (_iota((LANES, LANES), 0) < half) == (_iota((LANES, LANES), 1) < half)

    def load(ref):
        x = ref[...]
        if tb < c:
            x = jnp.concatenate([x, jnp.zeros((ns, c - tb, d), F32)], axis=1)
        return x.reshape(n, d)

    def by_pair(x):
        return jnp.stack([x[:, p * LANES:(p + 1) * LANES] for p in range(pairs)], axis=0)

    def stack(x):
        return jnp.concatenate([jnp.where(lane_a, x, 0.0), jnp.where(lane_a, 0.0, x)], axis=1)

    def unstack(x):
        return x[:, 0:n] + x[:, n:2 * n]

    r, ld, k, v, kk, a = (load(ref) for ref in (r_ref, ld_ref, k_ref, v_ref, kk_ref, a_ref))
    cum = _sel_mm3(tri, ld)
    cum_last = _sel_mm3(last, cum)
    g_end = jnp.exp(cum_last - cum)
    g_inv = jnp.exp(-cum)
    beta = kk * a
    a_t = by_pair(-kk * jnp.exp(cum - ld))
    r_t = by_pair(r * jnp.exp(cum))
    k_t = by_pair(k * g_inv)
    b_t = by_pair(beta * g_inv)
    at_st = stack(a_t)
    rt_st = stack(r_t)
    k2 = jnp.concatenate([k_t, k_t], axis=1)
    b2 = jnp.concatenate([b_t, b_t], axis=1)
    m_k = jnp.where(strict, _mm1(at_st, k2, _BNT), 0.0)
    m_b = jnp.where(strict, _mm1(at_st, b2, _BNT), 0.0)
    a_k = jnp.where(incl, _mm1(rt_st, k2, _BNT), 0.0)
    a_b = jnp.where(incl, _mm1(rt_st, b2, _BNT), 0.0)
    t_inv = _unit_lower_inverse(m_b, c)
    v_p = by_pair(v)
    v_st = stack(v_p)
    states = [s_scr[s] for s in range(ns)]
    w1 = jnp.concatenate([_mm1(a_t[:, s * c:(s + 1) * c], states[s], _BNT) for s in range(ns)], axis=1)
    o1 = jnp.concatenate([_mm1(r_t[:, s * c:(s + 1) * c], states[s], _BNT) for s in range(ns)], axis=1)
    sa_st = _mm1(t_inv, stack(w1) + _mm1(m_k, v_st, _BNN), _BNN)
    o = o1 + unstack(_mm1(a_k, v_st, _BNN) + _mm1(a_b, sa_st, _BNN))
    sa = unstack(sa_st)
    for p in range(pairs):
        o_ref[:, :, p * LANES:(p + 1) * LANES] = o[p].reshape(ns, c, LANES)[:, 0:tb, :]
    kg = by_pair(k * g_end)
    bg = by_pair(beta * g_end)
    for s in range(ns):
        rows = slice(s * c, (s + 1) * c)
        upd = _mm1(v_p[:, rows], kg[:, rows], _BTN) + _mm1(sa[:, rows], bg[:, rows], _BTN)
        decay = by_pair(jnp.exp(cum_last[s * c:s * c + 1, :]))
        s_scr[s] = states[s] * decay + jnp.where(bd, upd, 0.0)

    @pl.when(j == pl.num_programs(1) - 1)
    def _():
        for s in range(ns):
            for p in range(pairs):
                blk = s_scr[s, p]
                s_out_ref[s, 2 * p] = blk[0:half, 0:half]
                s_out_ref[s, 2 * p + 1] = blk[half:LANES, half:LANES]


def _rwkv_chunk(r, ld, k, v, kk, a, s0):
    B, T, d = r.shape
    heads = d // RW_HEAD
    if T >= 64:
        c, ns, tb = 64, 1, 64
    else:
        c, ns, tb = 16, 4, T
    assert T % tb == 0 and B % ns == 0 and tb <= c
    tok = pl.BlockSpec((ns, tb, d), lambda i, j: (i, j, 0))
    st = pl.BlockSpec((ns, heads, RW_HEAD, RW_HEAD), lambda i, j: (i, 0, 0, 0))
    return pl.pallas_call(
        functools.partial(_rwkv_chunk_kernel, c=c),
        grid=(B // ns, T // tb),
        in_specs=[tok] * 6 + [st],
        out_specs=[tok, st],
        out_shape=[jax.ShapeDtypeStruct((B, T, d), F32),
                   jax.ShapeDtypeStruct((B, heads, RW_HEAD, RW_HEAD), F32)],
        scratch_shapes=[pltpu.VMEM((ns, d // LANES, LANES, LANES), F32)],
        compiler_params=_params(("parallel", "arbitrary")),
        name="rwkv_chunk",
    )(r, ld, k, v, kk, a, s0)


def _residual_out(y, w_ref, x_ref, mod_ref, fg_ref, o_ref, final):
    bb, tt, d = x_ref.shape
    out = _mm1(y, w_ref[...]).reshape(bb, tt, d)
    x = x_ref[...] + mod_ref[:, :, 2 * d:3 * d] * out
    if final:
        x = x * lax.rsqrt(jnp.mean(x * x, axis=-1, keepdims=True) + NORM_EPS) * fg_ref[...]
    o_ref[...] = x


def _rwkv_post_kernel(o_ref, bonus_ref, g_ref, x_ref, mod_ref, gnw_ref, gnb_ref, wo_ref, fg_ref, out_ref, *, final):
    bb, tt, d = x_ref.shape
    m = bb * tt
    o = o_ref[...].reshape(m, d)
    mu = _head_bcast(_head_sum(o, RW_HEAD) * (1.0 / RW_HEAD), d, RW_HEAD)
    oc = o - mu
    var = _head_bcast(_head_sum(oc * oc, RW_HEAD) * (1.0 / RW_HEAD), d, RW_HEAD)
    y = oc * lax.rsqrt(var + RW_GN_EPS) * gnw_ref[...] + gnb_ref[...]
    y = (y + bonus_ref[...].reshape(m, d)) * g_ref[...].reshape(m, d)
    _residual_out(y, wo_ref, x_ref, mod_ref, fg_ref, out_ref, final)


def _post_call(kernel, name, x, mod, tok_inputs, consts, final):
    B, T, d = x.shape
    bb, tt = _row_tiles(B, T)
    tok = lambda a: pl.BlockSpec((bb, tt, a.shape[-1]), lambda i, j: (i, j, 0))
    return pl.pallas_call(
        functools.partial(kernel, final=final),
        grid=(B // bb, T // tt),
        in_specs=[tok(a) for a in tok_inputs] + [tok(x), pl.BlockSpec((bb, 1, 3 * d), lambda i, j: (i, 0, 0))]
        + [_const_spec(c.shape) for c in consts],
        out_specs=tok(x),
        out_shape=jax.ShapeDtypeStruct((B, T, d), F32),
        compiler_params=_params(("parallel", "parallel")),
        name=name,
    )(*tok_inputs, x, mod, *consts)


def _rwkv_layer(x, mod, norm_g, final_g, final, x_prev, s0, p):
    d = x.shape[-1]
    r, ld, k, v, kk, a, g, bonus, shift = _rwkv_pre(x, x_prev, mod, norm_g, p)
    o, s_new = _rwkv_chunk(r, ld, k, v, kk, a, s0)
    consts = [p['gn_w'].reshape(1, d), p['gn_b'].reshape(1, d), p['w_o'].astype(BF16), final_g.reshape(1, d)]
    x = _post_call(_rwkv_post_kernel, "rwkv_post", x, mod, [o, bonus, g], consts, final)
    return x, s_new, shift[:, 0, :]


def _gla_pre_kernel(x_ref, mod_ref, ng_ref, w_ref, wl_ref, w2_ref, b2_ref, q_out, k_out, v_out, gate_out, gk_out,
                    *, qk):
    bb, tt, d = x_ref.shape
    m = bb * tt
    h = _prenorm(x_ref[...], ng_ref[...], mod_ref[...], d).reshape(m, d).astype(BF16)
    dk = qk // GLA_HEADS
    q_out[...] = (_dot(h, w_ref[:, 0:qk]) * dk ** -0.5).reshape(bb, tt, qk)
    k_out[...] = _dot(h, w_ref[:, qk:2 * qk]).reshape(bb, tt, qk)
    v_out[...] = _dot(h, w_ref[:, 2 * qk:2 * qk + d]).reshape(bb, tt, d)
    gate_out[...] = _dot(h, w_ref[:, 2 * qk + d:2 * qk + 2 * d]).reshape(bb, tt, d)
    low = _dot(h, wl_ref[...])
    gk = _mm3(low, w2_ref[...]) + b2_ref[...]
    gk_out[...] = (-_softplus(-gk) * (1.0 / GLA_GATE_NORMALIZER)).reshape(bb, tt, qk)


def _hg_pre_kernel(x_ref, mod_ref, ng_ref, w_ref, lower_ref, q_out, k_out, v_out, gate_out, g_out, *, layer):
    bb, tt, d = x_ref.shape
    m = bb * tt
    h = _prenorm(x_ref[...], ng_ref[...], mod_ref[...], d).reshape(m, d).astype(BF16)
    dk = d // HG_HEADS
    low = lower_ref[...]
    e = jnp.exp(low - jnp.max(low, axis=0, keepdims=True))
    soft = e / jnp.sum(e, axis=0, keepdims=True)
    lb = jnp.sum(soft[0:layer + 1], axis=0, keepdims=True) - soft[0:1]
    q = _dot(h, w_ref[:, 0:d])
    f = _dot(h, w_ref[:, d:2 * d])
    forget = lb + (1.0 - lb) * _sigmoid(f)
    q_out[...] = (_silu(q) * dk ** -0.5).reshape(bb, tt, d)
    k_out[...] = (1.0 - forget).reshape(bb, tt, d)
    g_out[...] = jnp.log(forget).reshape(bb, tt, d)
    v_out[...] = _dot(h, w_ref[:, 2 * d:3 * d]).reshape(bb, tt, d)
    gate_out[...] = _dot(h, w_ref[:, 3 * d:4 * d]).reshape(bb, tt, d)


def _pre_call(kernel, name, x, mod, consts, out_widths):
    B, T, d = x.shape
    bb, tt = _row_tiles(B, T)
    tok = lambda n: pl.BlockSpec((bb, tt, n), lambda i, j: (i, j, 0))
    return pl.pallas_call(
        kernel,
        grid=(B // bb, T // tt),
        in_specs=[tok(d), pl.BlockSpec((bb, 1, 3 * d), lambda i, j: (i, 0, 0))]
        + [_const_spec(c.shape) for c in consts],
        out_specs=[tok(n) for n in out_widths],
        out_shape=[jax.ShapeDtypeStruct((B, T, n), F32) for n in out_widths],
        compiler_params=_params(("parallel", "parallel")),
        name=name,
    )(x, mod, *consts)


def _gla_chunk_kernel(q_ref, k_ref, g_ref, v_ref, s0_ref, o_ref, s_out_ref, st_scr, *, c, heads):
    tb = q_ref.shape[1]
    dk = q_ref.shape[2] // heads
    dv = v_ref.shape[2] // heads
    n_chunks = max(1, tb // c)
    rows = min(tb, c)
    j = pl.program_id(1)

    @pl.when(j == 0)
    def _():
        for h in range(heads):
            st_scr[h] = s0_ref[0, h].T

    tc = _iota((c, c), 0)
    sc = _iota((c, c), 1)
    tri = jnp.where(sc <= tc, 1.0, 0.0).astype(BF16)
    sub = min(GLA_SUB, c)
    key_row = _iota((c, 1), 0)
    half = sub // 2
    row = _iota((half, 1), 0)
    st = st_scr[...]
    for ci in range(n_chunks):
        def load(ref, w):
            x = ref[0, ci * c:ci * c + rows, :]
            if rows < c:
                x = jnp.concatenate([x, jnp.zeros((c - rows, x.shape[1]), F32)], axis=0)
            return jnp.stack([x[:, h * w:(h + 1) * w] for h in range(heads)], axis=0)

        q, k, g, v = load(q_ref, dk), load(k_ref, dk), load(g_ref, dk), load(v_ref, dv)
        b = jnp.stack([_sel_mm3(tri, g[h]) for h in range(heads)], axis=0)
        o_inter = _mm3(q * jnp.exp(b), st, _BNT)
        outs = []
        for i in range(c // sub):
            r0 = i * sub
            qi = q[:, r0:r0 + sub]
            bi = b[:, r0:r0 + sub]
            acc = o_inter[:, r0:r0 + sub]
            if i > 0:
                ref_b = b[:, r0 - 1:r0]
                k_fac = jnp.exp(jnp.where(key_row < r0, ref_b - b, -jnp.inf))
                att = _mm3(qi * jnp.exp(bi - ref_b), k * k_fac, _BNT)
                acc = acc + _mm3(att, v, _BNN)
            parts = [acc[:, 0:half], acc[:, half:sub]]
            for s in range(sub):
                k_s, v_s, b_s = k[:, r0 + s:r0 + s + 1], v[:, r0 + s:r0 + s + 1], bi[:, s:s + 1]
                for hf in range(2):
                    lo = hf * half
                    if s >= lo + half:
                        continue
                    diff = bi[:, lo:lo + half] - b_s
                    if s > lo:
                        diff = jnp.where(row >= s - lo, diff, -jnp.inf)
                    col = jnp.sum(qi[:, lo:lo + half] * k_s * jnp.exp(diff), axis=2, keepdims=True)
                    parts[hf] = parts[hf] + col * v_s
            outs.append(jnp.concatenate(parts, axis=1))
        o = jnp.concatenate(outs, axis=1) if len(outs) > 1 else outs[0]
        for h in range(heads):
            o_ref[0, ci * c:ci * c + rows, h * dv:(h + 1) * dv] = o[h, 0:rows]
        b_last = b[:, c - 1:c]
        st = st * jnp.exp(b_last) + _mm3(v, k * jnp.exp(b_last - b), _BTN)
    st_scr[...] = st

    @pl.when(j == pl.num_programs(1) - 1)
    def _():
        for h in range(heads):
            s_out_ref[0, h] = st[h].T


def _gla_chunk(q, k, g, v, s0, heads):
    B, T, qk = q.shape
    d = v.shape[-1]
    dk = qk // heads
    dv = d // heads
    if T % 128 == 0:
        tb, c = 128, 64
    elif T % 64 == 0:
        tb, c = 64, 64
    else:
        tb, c = T, -(-T // SUB) * SUB
    kspec = pl.BlockSpec((1, tb, qk), lambda b, j: (b, j, 0))
    vspec = pl.BlockSpec((1, tb, d), lambda b, j: (b, j, 0))
    sspec = pl.BlockSpec((1, heads, dk, dv), lambda b, j: (b, 0, 0, 0))
    return pl.pallas_call(
        functools.partial(_gla_chunk_kernel, c=c, heads=heads),
        grid=(B, T // tb),
        in_specs=[kspec, kspec, kspec, vspec, sspec],
        out_specs=[vspec, sspec],
        out_shape=[jax.ShapeDtypeStruct((B, T, d), F32),
                   jax.ShapeDtypeStruct((B, heads, dk, dv), F32)],
        scratch_shapes=[pltpu.VMEM((heads, dv, dk), F32)],
        compiler_params=_params(("parallel", "arbitrary")),
        name="gla_chunk",
    )(q, k, g, v, s0)


def _gla_post_kernel(o_ref, gate_ref, x_ref, mod_ref, gnw_ref, wo_ref, fg_ref, out_ref, *, final, heads):
    bb, tt, d = x_ref.shape
    m = bb * tt
    hs = d // heads
    o = o_ref[...].reshape(m, d)
    ms = _head_bcast(_head_sum(o * o, hs) * (1.0 / hs), d, hs)
    y = o * lax.rsqrt(ms + NORM_EPS) * gnw_ref[...]
    y = y * _silu(gate_ref[...].reshape(m, d))
    _residual_out(y, wo_ref, x_ref, mod_ref, fg_ref, out_ref, final)


def _gla_layer(x, mod, norm_g, final_g, final, s0, p):
    d = x.shape[-1]
    qk = p['w_gk2'].shape[1]
    w_in = p['w_in']
    consts = [norm_g.reshape(1, d), w_in[:, :2 * qk + 2 * d].astype(BF16),
              _pad_cols(w_in[:, 2 * qk + 2 * d:], LANES).astype(BF16), _pad_rows(p['w_gk2'], LANES),
              p['b_gk2'].reshape(1, qk)]
    q, k, v, gate, gk = _pre_call(functools.partial(_gla_pre_kernel, qk=qk), "gla_pre", x, mod, consts,
                                  (qk, qk, d, d, qk))
    o, s_new = _gla_chunk(q, k, gk, v, s0, GLA_HEADS)
    post_consts = [jnp.tile(p['gn_w'], GLA_HEADS).reshape(1, d), p['w_o'].astype(BF16), final_g.reshape(1, d)]
    x = _post_call(functools.partial(_gla_post_kernel, heads=GLA_HEADS), "gla_post", x, mod, [o, gate], post_consts,
                   final)
    return x, s_new


def _hg_layer(x, mod, norm_g, final_g, final, s0, layer, p):
    d = x.shape[-1]
    consts = [norm_g.reshape(1, d), p['w_in'].astype(BF16), p['lower']]
    q, k, v, gate, g = _pre_call(functools.partial(_hg_pre_kernel, layer=layer), "hgrn_pre", x, mod, consts,
                                 (d, d, d, d, d))
    o, s_new = _gla_chunk(q, k, g, v, s0, HG_HEADS)
    post_consts = [jnp.tile(p['gn_w'], HG_HEADS).reshape(1, d), p['w_o'].astype(BF16), final_g.reshape(1, d)]
    x = _post_call(functools.partial(_gla_post_kernel, heads=HG_HEADS), "hgrn_post", x, mod, [o, gate], post_consts,
                   final)
    return x, s_new


def _sb_pre_kernel(x_ref, mod_ref, ng_ref, w_ref, q_out, k_out, v_out, gate_out):
    bb, tt, d = x_ref.shape
    m = bb * tt
    h = _prenorm(x_ref[...], ng_ref[...], mod_ref[...], d).reshape(m, d).astype(BF16)
    hd = d // SB_HEADS
    q_out[...] = (_dot(h, w_ref[:, 0:d]) * hd ** -0.5).reshape(bb, tt, d)
    k_out[...] = _dot(h, w_ref[:, d:2 * d]).reshape(bb, tt, d)
    v_out[...] = _dot(h, w_ref[:, 2 * d:3 * d]).reshape(bb, tt, d)
    gate_out[...] = _dot(h, w_ref[:, 3 * d:4 * d]).reshape(bb, tt, d)


def _sb_prompt_kernel(bias_ref, q_ref, k_ref, v_ref, o_ref, acc_scr, z_scr, sp_scr, *, tq):
    p = pl.program_id(1)
    i = pl.program_id(2)
    half = LANES // 2
    q = q_ref[0] * LOG2E
    lane_a = _iota((tq, LANES), 1) < half
    q_st = jnp.concatenate([jnp.where(lane_a, q, 0.0), jnp.where(lane_a, 0.0, q)], axis=0).astype(BF16)
    bias = jnp.where(_iota((2 * tq, 1), 0) < tq, bias_ref[2 * p], bias_ref[2 * p + 1]) * LOG2E
    tri = jnp.where(_iota((tq, tq), 0) >= _iota((tq, tq), 1), 1.0, 0.0).astype(BF16)

    def block_rows(jb):
        return pl.ds(pl.multiple_of(jnp.maximum(jb, 0) * tq, tq), tq)

    def scores(jb):
        z2 = _dot(q_st, k_ref[0, block_rows(jb), :].astype(BF16), _NT) + bias
        sp2 = jnp.maximum(z2, 0.0) + jnp.log2(1.0 + jnp.exp2(-jnp.abs(z2)))
        return z2, sp2

    def score_into(slot, jb):
        z2, sp2 = scores(jb)
        z_scr[slot] = z2
        sp_scr[slot] = sp2.astype(BF16)

    def finish(slot, jb, carry):
        cs = _dot(sp_scr[slot], tri)
        w = jnp.exp2(z_scr[slot] - cs - carry)
        acc_scr[...] += _dot(w.astype(BF16), v_ref[0, block_rows(jb), :].astype(BF16))
        return carry + cs[:, 0:1]

    score_into(0, i - 1)

    z2, sp2 = scores(i)
    before = _iota((2 * tq, tq), 1) < _iota((2 * tq, tq), 0) % tq
    spm = jnp.where(before, sp2, 0.0).astype(BF16)
    cs = _dot(spm, tri)
    w = jnp.where(before, jnp.exp2((z2 - sp2) - (cs - spm.astype(F32))), 0.0)
    acc_scr[...] = _dot(w.astype(BF16), v_ref[0, block_rows(i), :].astype(BF16))
    carry = cs[:, 0:1]

    def pair(n, carry):
        jb = i - 1 - 2 * n
        score_into(1, jb - 1)
        carry = finish(0, jb, carry)
        score_into(0, jb - 2)
        return finish(1, jb - 1, carry)

    carry = lax.fori_loop(0, i // 2, pair, carry)

    @pl.when(i % 2 == 1)
    def _():
        finish(0, 0, carry)

    acc = acc_scr[...]
    o_ref[0] = jnp.where(lane_a, acc[0:tq], acc[tq:2 * tq])


def _sb_prompt(q, k, v, bias):
    B, T, d = q.shape
    tq = min(T, 256)
    qspec = pl.BlockSpec((1, tq, LANES), lambda b, p, i: (b, i, p))
    kspec = pl.BlockSpec((1, T, LANES), lambda b, p, i: (b, 0, p))
    return pl.pallas_call(
        functools.partial(_sb_prompt_kernel, tq=tq),
        grid=(B, d // LANES, T // tq),
        in_specs=[pl.BlockSpec(memory_space=pltpu.SMEM), qspec, kspec, kspec],
        out_specs=qspec,
        out_shape=jax.ShapeDtypeStruct((B, T, d), F32),
        scratch_shapes=[pltpu.VMEM((2 * tq, LANES), F32), pltpu.VMEM((2, 2 * tq, tq), F32),
                        pltpu.VMEM((2, 2 * tq, tq), BF16)],
        compiler_params=_params(("parallel", "parallel", "arbitrary")),
        name="sb_attention_prompt",
    )(bias, q, k, v)


def _sb_paged_kernel(pt_ref, q_ref, kn_ref, vn_ref, bias_ref, tri_ref, *refs, n_blocks):
    kc_refs, vc_refs = refs[:n_blocks], refs[n_blocks:2 * n_blocks]
    o_ref, acc_scr, carry_scr, qbd_scr = refs[2 * n_blocks:]
    t, d = q_ref.shape[1], q_ref.shape[2]
    heads = LANES // t
    hd = d // heads
    j = pl.program_id(1)
    bias = bias_ref[...] * LOG2E

    def attend(k, v, nk, before, carry, qk_dims, pv_dims):
        z2 = _dot(qbd_scr[...], k, qk_dims) + bias[:, 0:nk]
        sp2 = jnp.maximum(z2, 0.0) + jnp.log2(1.0 + jnp.exp2(-jnp.abs(z2)))
        spb = (sp2 if before is None else jnp.where(before, sp2, 0.0)).astype(BF16)
        cs = _dot(spb, tri_ref[0:nk, 0:nk])
        if before is None:
            w = jnp.exp2(z2 - cs - jnp.concatenate([carry] * (nk // LANES), axis=1))
        else:
            w = jnp.where(before, jnp.exp2((z2 - sp2) - (cs - spb.astype(F32))), 0.0)
        total = _dot(spb, jnp.ones((nk, LANES), BF16))
        return _dot(w.astype(BF16), v, pv_dims), total

    @pl.when(j == 0)
    def _():
        head_of_row = _iota((LANES, d), 0) // t
        head_of_lane = _iota((LANES, d), 1) // hd
        q_rows = jnp.concatenate([q_ref[0] * LOG2E] * heads, axis=0)
        qbd_scr[...] = jnp.where(head_of_lane == head_of_row, q_rows, 0.0).astype(BF16)
        nk = 2 * t
        pad = jnp.zeros((nk - t, d), F32)
        before = _iota((LANES, nk), 1) < _iota((LANES, nk), 0) % t
        upd, total = attend(jnp.concatenate([kn_ref[0], pad], axis=0).astype(BF16),
                            jnp.concatenate([vn_ref[0], pad], axis=0).astype(BF16), nk, before, None, _NT, _NN)
        acc_scr[...] = upd
        carry_scr[...] = total

    page = kc_refs[0].shape[2]

    def joined(page_refs):
        return jnp.concatenate([r[...].reshape(d, page) for r in reversed(page_refs)], axis=1).astype(BF16)

    upd, total = attend(joined(kc_refs), joined(vc_refs), n_blocks * page, None, carry_scr[...], _NN, _NT)
    acc_scr[...] += upd
    carry_scr[...] += total

    @pl.when(j == pl.num_programs(1) - 1)
    def _():
        full = acc_scr[...]
        head_of_lane = _iota((t, d), 1) // hd
        out = jnp.zeros((t, d), F32)
        for h in range(heads):
            out = jnp.where(head_of_lane == h, full[h * t:(h + 1) * t, :], out)
        o_ref[0] = out


def _sb_paged(q, k_new, v_new, bias, k_cache, v_cache, layer, page_table):
    B, T, d = q.shape
    _, _, page, heads, hd = k_cache.shape
    n_pages = page_table.shape[1]
    n_blocks = max(n for n in (8, 4, 2, 1) if n_pages % n == 0)
    assert heads * T == LANES and heads * hd == d
    tok = pl.BlockSpec((1, T, d), lambda b, j, pt: (b, 0, 0))

    def cache(i):
        return pl.BlockSpec((None, None, heads, hd, page),
                            lambda b, j, pt: (layer, pt[b, n_pages - 1 - (j * n_blocks + i)], 0, 0, 0))

    nk = n_blocks * page
    bias_rows = jnp.broadcast_to(jnp.repeat(bias, T)[:, None], (LANES, nk))
    tri = (jnp.arange(nk)[:, None] >= jnp.arange(nk)[None, :]).astype(BF16)
    grid_spec = pltpu.PrefetchScalarGridSpec(
        num_scalar_prefetch=1,
        grid=(B, n_pages // n_blocks),
        in_specs=[tok, tok, tok, pl.BlockSpec((LANES, nk), lambda b, j, pt: (0, 0)),
                  pl.BlockSpec((nk, nk), lambda b, j, pt: (0, 0))]
        + [cache(i) for i in range(n_blocks)] * 2,
        out_specs=tok,
        scratch_shapes=[pltpu.VMEM((LANES, d), F32), pltpu.VMEM((LANES, LANES), F32), pltpu.VMEM((LANES, d), BF16)],
    )
    return pl.pallas_call(
        functools.partial(_sb_paged_kernel, n_blocks=n_blocks),
        grid_spec=grid_spec,
        out_shape=jax.ShapeDtypeStruct((B, T, d), F32),
        compiler_params=_params(("parallel", "arbitrary")),
        name="sb_attention_paged",
    )(page_table, q, k_new, v_new, bias_rows, tri, *([jnp.transpose(k_cache, (0, 1, 3, 4, 2))] * n_blocks),
      *([jnp.transpose(v_cache, (0, 1, 3, 4, 2))] * n_blocks))


def _sb_post_kernel(o_ref, gate_ref, x_ref, mod_ref, wo_ref, fg_ref, out_ref, *, final):
    bb, tt, d = x_ref.shape
    m = bb * tt
    y = o_ref[...].reshape(m, d) * _silu(gate_ref[...].reshape(m, d))
    _residual_out(y, wo_ref, x_ref, mod_ref, fg_ref, out_ref, final)


def _sb_layer(x, mod, norm_g, final_g, final, past, p):
    B, T, d = x.shape
    consts = [norm_g.reshape(1, d), p['w_in'].astype(BF16)]
    q, k, v, gate = _pre_call(_sb_pre_kernel, "sb_pre", x, mod, consts, (d, d, d, d))
    if past is None:
        o = _sb_prompt(q, k, v, p['bias'])
    else:
        o = _sb_paged(q, k, v, p['bias'], *past)
    x = _post_call(_sb_post_kernel, "sb_post", x, mod, [o, gate], [p['w_o'].astype(BF16), final_g.reshape(1, d)],
                   final)
    hd = d // SB_HEADS
    return x, k.reshape(B, T, SB_HEADS, hd), v.reshape(B, T, SB_HEADS, hd)


def kernel(x_prompt, x_sample, c_prompt, c_sample, state_rwkv, cache_rwkv_shift, state_gla, cache_sb_k, cache_sb_v, state_hgrn, page_table, norm_g, ada_w, ada_b, final_g, rw_mix, rw_w_rkvg, rw_w0, rw_w1, rw_w2, rw_a0, rw_a1, rw_a2, rw_k_k, rw_k_a, rw_r_k, rw_gn_w, rw_gn_b, rw_w_o, gla_w_in, gla_w_gk2, gla_b_gk2, gla_gn_w, gla_w_o, sb_w_in, sb_bias, sb_w_o, hg_w_in, hg_lower, hg_gn_w, hg_w_o):
    depth, d = norm_g.shape
    n_mix = 4
    bp, bs = x_prompt.shape[0], x_sample.shape[0]
    rows = -(-(bp + bs) // 8) * 8
    c_all = jnp.pad(jnp.concatenate([c_prompt, c_sample], axis=0), ((0, rows - bp - bs), (0, 0)))
    mods = _modulation(c_all, ada_w, ada_b)

    def trunk(x, mod_all, rw_s, rw_shift, gla_s, sb_past, hg_s):
        B = x.shape[0]
        outs = {n: [] for n in ('rw_s', 'rw_shift', 'gla_s', 'sb_k', 'sb_v', 'hg_s')}
        for i in range(depth):
            kind, j = i % n_mix, i // n_mix
            mod = mod_all[i].reshape(B, 1, 3 * d)
            final = i == depth - 1
            if kind == 0:
                p = dict(mix=rw_mix[j], w_rkvg=rw_w_rkvg[j], w0=rw_w0[j], w1=rw_w1[j], w2=rw_w2[j], a0=rw_a0[j],
                         a1=rw_a1[j], a2=rw_a2[j], k_k=rw_k_k[j], k_a=rw_k_a[j], r_k=rw_r_k[j], gn_w=rw_gn_w[j],
                         gn_b=rw_gn_b[j], w_o=rw_w_o[j])
                x, s, sh = _rwkv_layer(x, mod, norm_g[i], final_g, final, rw_shift[j], rw_s[j], p)
                outs['rw_s'].append(s)
                outs['rw_shift'].append(sh)
            elif kind == 1:
                p = dict(w_in=gla_w_in[j], w_gk2=gla_w_gk2[j], b_gk2=gla_b_gk2[j], gn_w=gla_gn_w[j], w_o=gla_w_o[j])
                x, s = _gla_layer(x, mod, norm_g[i], final_g, final, gla_s[j], p)
                outs['gla_s'].append(s)
            elif kind == 2:
                p = dict(w_in=sb_w_in[j], bias=sb_bias[j], w_o=sb_w_o[j])
                x, k_new, v_new = _sb_layer(x, mod, norm_g[i], final_g, final, sb_past(j), p)
                outs['sb_k'].append(k_new)
                outs['sb_v'].append(v_new)
            else:
                p = dict(w_in=hg_w_in[j], lower=hg_lower, gn_w=hg_gn_w[j], w_o=hg_w_o[j])
                x, s = _hg_layer(x, mod, norm_g[i], final_g, final, hg_s[j], i, p)
                outs['hg_s'].append(s)
        st = lambda n: jnp.stack(outs[n])
        return x, st('rw_s'), st('rw_shift'), st('gla_s'), st('sb_k'), st('sb_v'), st('hg_s')

    z = lambda a, b: jnp.zeros((a.shape[0], b) + a.shape[2:], a.dtype)
    prompt = trunk(x_prompt, mods[:, :bp], z(state_rwkv, bp), z(cache_rwkv_shift, bp), z(state_gla, bp),
                   lambda j: None, z(state_hgrn, bp))
    sample = trunk(x_sample, mods[:, bp:bp + bs], state_rwkv, cache_rwkv_shift, state_gla,
                   lambda j: (cache_sb_k, cache_sb_v, j, page_table), state_hgrn)
    return (prompt[0], sample[0]) + prompt[1:] + sample[1:]
```

```python
import functools
import math

import jax
import jax.numpy as jnp
from jax import lax
from jax.experimental import pallas as pl
from jax.experimental.pallas import tpu as pltpu

F32 = jnp.float32
BF16 = jnp.bfloat16

NORM_EPS = 1e-6
RW_HEAD = 64
RW_GN_EPS = 64e-5
GLA_HEADS = 4
GLA_GATE_NORMALIZER = 16.0
SB_HEADS = 16
HG_HEADS = 8
LANES = 128
ROW_TILE = 512
SUB = 16
GLA_SUB = 16
VMEM_LIMIT = 56 * 1024 * 1024
LOG2E = 1.4426950408889634

_NN = (((1,), (0,)), ((), ()))
_NT = (((1,), (1,)), ((), ()))
_TN = (((0,), (0,)), ((), ()))
_BNN = (((2,), (1,)), ((0,), (0,)))
_BNT = (((2,), (2,)), ((0,), (0,)))
_BTN = (((1,), (1,)), ((0,), (0,)))


def _dot(a, b, dims=_NN):
    return lax.dot_general(a, b, dims, preferred_element_type=F32)


def _split(x):
    hi = x.astype(BF16)
    lo = (x - hi.astype(F32)).astype(BF16)
    return hi, lo


def _mm1(a, b, dims=_NN):
    return _dot(a.astype(BF16), b.astype(BF16), dims)


def _mm3(a, b, dims=_NN):
    ah, al = _split(a)
    bh, bl = _split(b)
    return _dot(ah, bh, dims) + (_dot(ah, bl, dims) + _dot(al, bh, dims))


def _mm_sel(a, sel, dims=_NN):
    ah, al = _split(a)
    return _dot(ah, sel, dims) + _dot(al, sel, dims)


def _mm_sel_fused(a, sel2):
    ah, al = _split(a)
    return _dot(jnp.concatenate([ah, al], axis=1), sel2)


def _sel_mm(sel, b, dims=_NN):
    bh, bl = _split(b)
    return _dot(sel, bh, dims) + _dot(sel, bl, dims)


def _sel_mm3(sel, b):
    b1 = b.astype(BF16)
    r1 = b - b1.astype(F32)
    b2 = r1.astype(BF16)
    b3 = (r1 - b2.astype(F32)).astype(BF16)
    return _dot(sel, b1) + (_dot(sel, b2) + _dot(sel, b3))


def _silu(x):
    return x * (1.0 / (1.0 + jnp.exp(-x)))


def _sigmoid(x):
    return 1.0 / (1.0 + jnp.exp(-x))


def _softplus(x):
    return jnp.maximum(x, 0.0) + jnp.log1p(jnp.exp(-jnp.abs(x)))


def _iota(shape, dim):
    return lax.broadcasted_iota(jnp.int32, shape, dim)


def _head_indicator(d, hs):
    return jnp.where(_iota((d, LANES), 0) // hs == _iota((d, LANES), 1), 1.0, 0.0).astype(BF16)


def _head_indicator_t(d, hs):
    return jnp.where(_iota((LANES, d), 1) // hs == _iota((LANES, d), 0), 1.0, 0.0).astype(BF16)


def _head_sum(x, hs):
    return _mm_sel(x, _head_indicator(x.shape[-1], hs))


def _head_bcast(s, d, hs):
    return _mm_sel(s, _head_indicator_t(d, hs))


def _row_tiles(B, T):
    tt = min(T, ROW_TILE)
    bb = max(1, min(B, ROW_TILE // tt))
    assert T % tt == 0 and B % bb == 0 and tt % 8 == 0
    return bb, tt


def _params(sem):
    return pltpu.CompilerParams(dimension_semantics=sem, vmem_limit_bytes=VMEM_LIMIT)


def _const_spec(shape):
    n = len(shape)
    return pl.BlockSpec(shape, lambda *_: (0,) * n)


def _prenorm(x, g, mod, d):
    shift = mod[:, :, 0:d]
    scale = mod[:, :, d:2 * d]
    y = x * lax.rsqrt(jnp.mean(x * x, axis=-1, keepdims=True) + NORM_EPS)
    return (y * g) * (1.0 + scale) + shift


def _mod_kernel(c_ref, w_ref, b_ref, o_ref):
    o_ref[0] = _mm1(_silu(c_ref[...]), w_ref[0]) + b_ref[0]


def _modulation(c, ada_w, ada_b):
    depth, d, n = ada_w.shape
    rows = c.shape[0]
    tn = 1536
    return pl.pallas_call(
        _mod_kernel,
        grid=(depth, n // tn),
        in_specs=[pl.BlockSpec((rows, d), lambda l, j: (0, 0)),
                  pl.BlockSpec((1, d, tn), lambda l, j: (l, 0, j)),
                  pl.BlockSpec((1, 1, tn), lambda l, j: (l, 0, j))],
        out_specs=pl.BlockSpec((1, rows, tn), lambda l, j: (l, 0, j)),
        out_shape=jax.ShapeDtypeStruct((depth, rows, n), F32),
        compiler_params=_params(("parallel", "parallel")),
        name="ada_modulation",
    )(c, ada_w, ada_b.reshape(depth, 1, n))


def _rwkv_pre_kernel(x_ref, halo_ref, xprev_ref, mod_ref, ng_ref, mix_ref, wr_ref, wk_ref, wv_ref, wg_ref,
                     w0_ref, w1_ref, w2_ref, a0_ref, a1_ref, a2_ref, kk_ref, ka_ref, rk_ref,
                     r_out, ld_out, k_out, v_out, kkn_out, a_out, g_out, bonus_out, shift_out):
    bb, tt, d = x_ref.shape
    m = bb * tt
    mod = mod_ref[...]
    g = ng_ref[...]
    h3 = _prenorm(x_ref[...], g, mod, d)
    h_halo = _prenorm(halo_ref[:, 7:8, :], g, mod, d)
    h_prev = jnp.where(pl.program_id(1) == 0, xprev_ref[...], h_halo)
    shift_out[...] = h3[:, tt - 1:tt, :]
    h = h3.reshape(m, d)
    prev = jnp.where(_iota((m, 1), 0) % tt == 0,
                     jnp.broadcast_to(h_prev, (bb, tt, d)).reshape(m, d),
                     pltpu.roll(h, 1, axis=0))
    xx = prev - h
    xr, xw, xk, xv, xa, xg = (h + xx * mix_ref[n:n + 1, :] for n in range(6))
    r = _mm1(xr, wr_ref[...])
    k = _mm1(xk, wk_ref[...])
    v = _mm1(xv, wv_ref[...])
    gate = _silu(_mm1(xg, wg_ref[...]))
    lw = w0_ref[...] + _mm1(jnp.tanh(_mm1(xw, w1_ref[...])), w2_ref[...])
    log_w = -_softplus(-lw) - 0.5
    a = _sigmoid(a0_ref[...] + _mm1(_mm1(xa, a1_ref[...]), a2_ref[...]))
    kk = k * kk_ref[...]
    nrm = jnp.maximum(jnp.sqrt(_head_sum(kk * kk, RW_HEAD)), 1e-12)
    kk = kk * _head_bcast(1.0 / nrm, d, RW_HEAD)
    k = k * (1.0 + (a - 1.0) * ka_ref[...])
    bonus = _head_bcast(_head_sum(r * k * rk_ref[...], RW_HEAD), d, RW_HEAD) * v
    for ref, val in ((r_out, r), (ld_out, -jnp.exp(log_w)), (k_out, k), (v_out, v), (kkn_out, kk),
                     (a_out, a), (g_out, gate), (bonus_out, bonus)):
        ref[...] = val.reshape(bb, tt, d)


def _pad_cols(w, n):
    return jnp.pad(w, ((0, 0), (0, n - w.shape[1])))


def _pad_rows(w, n):
    return jnp.pad(w, ((0, n - w.shape[0]), (0, 0)))


def _rwkv_pre(x, x_prev, mod, norm_g, p):
    B, T, d = x.shape
    tt = min(T, ROW_TILE // 2)
    bb = max(1, min(B, (ROW_TILE // 2) // tt))
    row = lambda a: a.reshape(1, d)
    tok = pl.BlockSpec((bb, tt, d), lambda i, j: (i, j, 0))
    per_seq = lambda n: pl.BlockSpec((bb, 1, n), lambda i, j: (i, 0, 0))
    halo = pl.BlockSpec((bb, 8, d), lambda i, j: (i, jnp.maximum(j * (tt // 8) - 1, 0), 0))
    consts = [row(norm_g), p['mix'],
              p['w_rkvg'][0].astype(BF16), p['w_rkvg'][1].astype(BF16), p['w_rkvg'][2].astype(BF16),
              p['w_rkvg'][3].astype(BF16),
              row(p['w0']), _pad_cols(p['w1'], LANES).astype(BF16), _pad_rows(p['w2'], LANES).astype(BF16),
              row(p['a0']), _pad_cols(p['a1'], LANES).astype(BF16), _pad_rows(p['a2'], LANES).astype(BF16),
              row(p['k_k']), row(p['k_a']), row(p['r_k'])]
    out = pl.pallas_call(
        _rwkv_pre_kernel,
        grid=(B // bb, T // tt),
        in_specs=[tok, halo, per_seq(d), per_seq(3 * d)] + [_const_spec(c.shape) for c in consts],
        out_specs=[tok] * 8 + [per_seq(d)],
        out_shape=[jax.ShapeDtypeStruct((B, T, d), F32)] * 8 + [jax.ShapeDtypeStruct((B, 1, d), F32)],
        compiler_params=_params(("parallel", "arbitrary")),
        name="rwkv_pre",
    )(x, x, x_prev.reshape(B, 1, d), mod, *consts)
    return out


def _unit_lower_inverse(m_strict, c):
    n = m_strict.shape[-1]
    ti = _iota((n, n), 0)
    si = _iota((n, n), 1)
    eye = jnp.where(ti == si, 1.0, 0.0)
    sub = min(SUB, c)
    d_part = jnp.where(ti // sub == si // sub, m_strict, 0.0)
    t_d = eye + d_part
    pw = d_part
    for _ in range(int(math.log2(sub)) - 1):
        pw = _mm3(pw, pw, _BNN)
        t_d = t_d + _mm3(t_d, pw, _BNN)
    if sub == c:
        return t_d
    assert c // sub == 4
    nn = _mm3(t_d, m_strict - d_part, _BNN)
    n2 = _mm3(nn, nn, _BNN)
    return _mm3(eye + nn + n2 + _mm3(nn, n2, _BNN), t_d, _BNN)


def _rwkv_chunk_kernel(r_ref, ld_ref, k_ref, v_ref, kk_ref, a_ref, s0_ref, o_ref, s_out_ref, s_scr, *, c):
    ns, tb, d = r_ref.shape
    n = ns * c
    pairs = d // LANES
    half = LANES // 2
    j = pl.program_id(1)

    @pl.when(j == 0)
    def _():
        z = jnp.zeros((half, half), F32)
        for s in range(ns):
            for p in range(pairs):
                top = jnp.concatenate([s0_ref[s, 2 * p], z], axis=1)
                bot = jnp.concatenate([z, s0_ref[s, 2 * p + 1]], axis=1)
                s_scr[s, p] = jnp.concatenate([top, bot], axis=0)

    lane_a = _iota((n, LANES), 1) < half
    ti = _iota((2 * n, 2 * n), 0)
    si = _iota((2 * n, 2 * n), 1)
    same = ti // c == si // c
    strict = same & (si < ti)
    incl = same & (si <= ti)
    tc = _iota((n, n), 0)
    sc = _iota((n, n), 1)
    tri = jnp.where((tc // c == sc // c) & (sc <= tc), 1.0, 0.0).astype(BF16)
    last = jnp.where((tc // c == sc // c) & (sc % c == c - 1), 1.0, 0.0).astype(BF16)
    bd = (_iota((LANES, LANES), 0) < half) == (_iota((LANES, LANES), 1) < half)

    def load(ref):
        x = ref[...]
        if tb < c:
            x = jnp.concatenate([x, jnp.zeros((ns, c - tb, d), F32)], axis=1)
        return x.reshape(n, d)

    def by_pair(x):
        return jnp.stack([x[:, p * LANES:(p + 1) * LANES] for p in range(pairs)], axis=0)

    def stack(x):
        return jnp.concatenate([jnp.where(lane_a, x, 0.0), jnp.where(lane_a, 0.0, x)], axis=1)

    def unstack(x):
        return x[:, 0:n] + x[:, n:2 * n]

    r, ld, k, v, kk, a = (load(ref) for ref in (r_ref, ld_ref, k_ref, v_ref, kk_ref, a_ref))
    cum = _sel_mm3(tri, ld)
    cum_last = _sel_mm3(last, cum)
    g_end = jnp.exp(cum_last - cum)
    g_inv = jnp.exp(-cum)
    beta = kk * a
    a_t = by_pair(-kk * jnp.exp(cum - ld))
    r_t = by_pair(r * jnp.exp(cum))
    k_t = by_pair(k * g_inv)
    b_t = by_pair(beta * g_inv)
    at_st = stack(a_t)
    rt_st = stack(r_t)
    k2 = jnp.concatenate([k_t, k_t], axis=1)
    b2 = jnp.concatenate([b_t, b_t], axis=1)
    m_k = jnp.where(strict, _mm1(at_st, k2, _BNT), 0.0)
    m_b = jnp.where(strict, _mm1(at_st, b2, _BNT), 0.0)
    a_k = jnp.where(incl, _mm1(rt_st, k2, _BNT), 0.0)
    a_b = jnp.where(incl, _mm1(rt_st, b2, _BNT), 0.0)
    t_inv = _unit_lower_inverse(m_b, c)
    v_p = by_pair(v)
    v_st = stack(v_p)
    states = [s_scr[s] for s in range(ns)]
    w1 = jnp.concatenate([_mm1(a_t[:, s * c:(s + 1) * c], states[s], _BNT) for s in range(ns)], axis=1)
    o1 = jnp.concatenate([_mm1(r_t[:, s * c:(s + 1) * c], states[s], _BNT) for s in range(ns)], axis=1)
    sa_st = _mm1(t_inv, stack(w1) + _mm1(m_k, v_st, _BNN), _BNN)
    o = o1 + unstack(_mm1(a_k, v_st, _BNN) + _mm1(a_b, sa_st, _BNN))
    sa = unstack(sa_st)
    for p in range(pairs):
        o_ref[:, :, p * LANES:(p + 1) * LANES] = o[p].reshape(ns, c, LANES)[:, 0:tb, :]
    kg = by_pair(k * g_end)
    bg = by_pair(beta * g_end)
    for s in range(ns):
        rows = slice(s * c, (s + 1) * c)
        upd = _mm1(v_p[:, rows], kg[:, rows], _BTN) + _mm1(sa[:, rows], bg[:, rows], _BTN)
        decay = by_pair(jnp.exp(cum_last[s * c:s * c + 1, :]))
        s_scr[s] = states[s] * decay + jnp.where(bd, upd, 0.0)

    @pl.when(j == pl.num_programs(1) - 1)
    def _():
        for s in range(ns):
            for p in range(pairs):
                blk = s_scr[s, p]
                s_out_ref[s, 2 * p] = blk[0:half, 0:half]
                s_out_ref[s, 2 * p + 1] = blk[half:LANES, half:LANES]


def _rwkv_chunk(r, ld, k, v, kk, a, s0):
    B, T, d = r.shape
    heads = d // RW_HEAD
    if T >= 64:
        c, ns, tb = 64, 1, 64
    else:
        c, ns, tb = 16, 4, T
    assert T % tb == 0 and B % ns == 0 and tb <= c
    tok = pl.BlockSpec((ns, tb, d), lambda i, j: (i, j, 0))
    st = pl.BlockSpec((ns, heads, RW_HEAD, RW_HEAD), lambda i, j: (i, 0, 0, 0))
    return pl.pallas_call(
        functools.partial(_rwkv_chunk_kernel, c=c),
        grid=(B // ns, T // tb),
        in_specs=[tok] * 6 + [st],
        out_specs=[tok, st],
        out_shape=[jax.ShapeDtypeStruct((B, T, d), F32),
                   jax.ShapeDtypeStruct((B, heads, RW_HEAD, RW_HEAD), F32)],
        scratch_shapes=[pltpu.VMEM((ns, d // LANES, LANES, LANES), F32)],
        compiler_params=_params(("parallel", "arbitrary")),
        name="rwkv_chunk",
    )(r, ld, k, v, kk, a, s0)


def _residual_out(y, w_ref, x_ref, mod_ref, fg_ref, o_ref, final):
    bb, tt, d = x_ref.shape
    out = _mm1(y, w_ref[...]).reshape(bb, tt, d)
    x = x_ref[...] + mod_ref[:, :, 2 * d:3 * d] * out
    if final:
        x = x * lax.rsqrt(jnp.mean(x * x, axis=-1, keepdims=True) + NORM_EPS) * fg_ref[...]
    o_ref[...] = x


def _rwkv_post_kernel(o_ref, bonus_ref, g_ref, x_ref, mod_ref, gnw_ref, gnb_ref, wo_ref, fg_ref, out_ref, *, final):
    bb, tt, d = x_ref.shape
    m = bb * tt
    o = o_ref[...].reshape(m, d)
    mu = _head_bcast(_head_sum(o, RW_HEAD) * (1.0 / RW_HEAD), d, RW_HEAD)
    oc = o - mu
    var = _head_bcast(_head_sum(oc * oc, RW_HEAD) * (1.0 / RW_HEAD), d, RW_HEAD)
    y = oc * lax.rsqrt(var + RW_GN_EPS) * gnw_ref[...] + gnb_ref[...]
    y = (y + bonus_ref[...].reshape(m, d)) * g_ref[...].reshape(m, d)
    _residual_out(y, wo_ref, x_ref, mod_ref, fg_ref, out_ref, final)


def _post_call(kernel, name, x, mod, tok_inputs, consts, final):
    B, T, d = x.shape
    bb, tt = _row_tiles(B, T)
    tok = lambda a: pl.BlockSpec((bb, tt, a.shape[-1]), lambda i, j: (i, j, 0))
    return pl.pallas_call(
        functools.partial(kernel, final=final),
        grid=(B // bb, T // tt),
        in_specs=[tok(a) for a in tok_inputs] + [tok(x), pl.BlockSpec((bb, 1, 3 * d), lambda i, j: (i, 0, 0))]
        + [_const_spec(c.shape) for c in consts],
        out_specs=tok(x),
        out_shape=jax.ShapeDtypeStruct((B, T, d), F32),
        compiler_params=_params(("parallel", "parallel")),
        name=name,
    )(*tok_inputs, x, mod, *consts)


def _rwkv_layer(x, mod, norm_g, final_g, final, x_prev, s0, p):
    d = x.shape[-1]
    r, ld, k, v, kk, a, g, bonus, shift = _rwkv_pre(x, x_prev, mod, norm_g, p)
    o, s_new = _rwkv_chunk(r, ld, k, v, kk, a, s0)
    consts = [p['gn_w'].reshape(1, d), p['gn_b'].reshape(1, d), p['w_o'].astype(BF16), final_g.reshape(1, d)]
    x = _post_call(_rwkv_post_kernel, "rwkv_post", x, mod, [o, bonus, g], consts, final)
    return x, s_new, shift[:, 0, :]


def _gla_pre_kernel(x_ref, mod_ref, ng_ref, w_ref, wl_ref, w2_ref, b2_ref, q_out, k_out, v_out, gate_out, gk_out,
                    *, qk):
    bb, tt, d = x_ref.shape
    m = bb * tt
    h = _prenorm(x_ref[...], ng_ref[...], mod_ref[...], d).reshape(m, d).astype(BF16)
    dk = qk // GLA_HEADS
    q_out[...] = (_dot(h, w_ref[:, 0:qk]) * dk ** -0.5).reshape(bb, tt, qk)
    k_out[...] = _dot(h, w_ref[:, qk:2 * qk]).reshape(bb, tt, qk)
    v_out[...] = _dot(h, w_ref[:, 2 * qk:2 * qk + d]).reshape(bb, tt, d)
    gate_out[...] = _dot(h, w_ref[:, 2 * qk + d:2 * qk + 2 * d]).reshape(bb, tt, d)
    low = _dot(h, wl_ref[...])
    gk = _mm3(low, w2_ref[...]) + b2_ref[...]
    gk_out[...] = (-_softplus(-gk) * (1.0 / GLA_GATE_NORMALIZER)).reshape(bb, tt, qk)


def _hg_pre_kernel(x_ref, mod_ref, ng_ref, w_ref, lower_ref, q_out, k_out, v_out, gate_out, g_out, *, layer):
    bb, tt, d = x_ref.shape
    m = bb * tt
    h = _prenorm(x_ref[...], ng_ref[...], mod_ref[...], d).reshape(m, d).astype(BF16)
    dk = d // HG_HEADS
    low = lower_ref[...]
    e = jnp.exp(low - jnp.max(low, axis=0, keepdims=True))
    soft = e / jnp.sum(e, axis=0, keepdims=True)
    lb = jnp.sum(soft[0:layer + 1], axis=0, keepdims=True) - soft[0:1]
    q = _dot(h, w_ref[:, 0:d])
    f = _dot(h, w_ref[:, d:2 * d])
    forget = lb + (1.0 - lb) * _sigmoid(f)
    q_out[...] = (_silu(q) * dk ** -0.5).reshape(bb, tt, d)
    k_out[...] = (1.0 - forget).reshape(bb, tt, d)
    g_out[...] = jnp.log(forget).reshape(bb, tt, d)
    v_out[...] = _dot(h, w_ref[:, 2 * d:3 * d]).reshape(bb, tt, d)
    gate_out[...] = _dot(h, w_ref[:, 3 * d:4 * d]).reshape(bb, tt, d)


def _pre_call(kernel, name, x, mod, consts, out_widths):
    B, T, d = x.shape
    bb, tt = _row_tiles(B, T)
    tok = lambda n: pl.BlockSpec((bb, tt, n), lambda i, j: (i, j, 0))
    return pl.pallas_call(
        kernel,
        grid=(B // bb, T // tt),
        in_specs=[tok(d), pl.BlockSpec((bb, 1, 3 * d), lambda i, j: (i, 0, 0))]
        + [_const_spec(c.shape) for c in consts],
        out_specs=[tok(n) for n in out_widths],
        out_shape=[jax.ShapeDtypeStruct((B, T, n), F32) for n in out_widths],
        compiler_params=_params(("parallel", "parallel")),
        name=name,
    )(x, mod, *consts)


def _gla_chunk_kernel(q_ref, k_ref, g_ref, v_ref, s0_ref, o_ref, s_out_ref, st_scr, *, c, heads):
    tb = q_ref.shape[1]
    dk = q_ref.shape[2] // heads
    dv = v_ref.shape[2] // heads
    n_chunks = max(1, tb // c)
    rows = min(tb, c)
    j = pl.program_id(1)

    @pl.when(j == 0)
    def _():
        for h in range(heads):
            st_scr[h] = s0_ref[0, h].T

    tc = _iota((c, c), 0)
    sc = _iota((c, c), 1)
    tri = jnp.where(sc <= tc, 1.0, 0.0).astype(BF16)
    sub = min(GLA_SUB, c)
    key_row = _iota((c, 1), 0)
    half = sub // 2
    row = _iota((half, 1), 0)
    st = st_scr[...]
    for ci in range(n_chunks):
        def load(ref, w):
            x = ref[0, ci * c:ci * c + rows, :]
            if rows < c:
                x = jnp.concatenate([x, jnp.zeros((c - rows, x.shape[1]), F32)], axis=0)
            return jnp.stack([x[:, h * w:(h + 1) * w] for h in range(heads)], axis=0)

        q, k, g, v = load(q_ref, dk), load(k_ref, dk), load(g_ref, dk), load(v_ref, dv)
        b = jnp.stack([_sel_mm3(tri, g[h]) for h in range(heads)], axis=0)
        o_inter = _mm3(q * jnp.exp(b), st, _BNT)
        outs = []
        for i in range(c // sub):
            r0 = i * sub
            qi = q[:, r0:r0 + sub]
            bi = b[:, r0:r0 + sub]
            acc = o_inter[:, r0:r0 + sub]
            if i > 0:
                ref_b = b[:, r0 - 1:r0]
                k_fac = jnp.exp(jnp.where(key_row < r0, ref_b - b, -jnp.inf))
                att = _mm3(qi * jnp.exp(bi - ref_b), k * k_fac, _BNT)
                acc = acc + _mm3(att, v, _BNN)
            parts = [acc[:, 0:half], acc[:, half:sub]]
            for s in range(sub):
                k_s, v_s, b_s = k[:, r0 + s:r0 + s + 1], v[:, r0 + s:r0 + s + 1], bi[:, s:s + 1]
                for hf in range(2):
                    lo = hf * half
                    if s >= lo + half:
                        continue
                    diff = bi[:, lo:lo + half] - b_s
                    if s > lo:
                        diff = jnp.where(row >= s - lo, diff, -jnp.inf)
                    col = jnp.sum(qi[:, lo:lo + half] * k_s * jnp.exp(diff), axis=2, keepdims=True)
                    parts[hf] = parts[hf] + col * v_s
            outs.append(jnp.concatenate(parts, axis=1))
        o = jnp.concatenate(outs, axis=1) if len(outs) > 1 else outs[0]
        for h in range(heads):
            o_ref[0, ci * c:ci * c + rows, h * dv:(h + 1) * dv] = o[h, 0:rows]
        b_last = b[:, c - 1:c]
        st = st * jnp.exp(b_last) + _mm3(v, k * jnp.exp(b_last - b), _BTN)
    st_scr[...] = st

    @pl.when(j == pl.num_programs(1) - 1)
    def _():
        for h in range(heads):
            s_out_ref[0, h] = st[h].T


def _gla_chunk(q, k, g, v, s0, heads):
    B, T, qk = q.shape
    d = v.shape[-1]
    dk = qk // heads
    dv = d // heads
    if T % 256 == 0:
        tb, c = 256, 64
    elif T % 128 == 0:
        tb, c = 128, 64
    elif T % 64 == 0:
        tb, c = 64, 64
    else:
        tb, c = T, -(-T // SUB) * SUB
    kspec = pl.BlockSpec((1, tb, qk), lambda b, j: (b, j, 0))
    vspec = pl.BlockSpec((1, tb, d), lambda b, j: (b, j, 0))
    sspec = pl.BlockSpec((1, heads, dk, dv), lambda b, j: (b, 0, 0, 0))
    return pl.pallas_call(
        functools.partial(_gla_chunk_kernel, c=c, heads=heads),
        grid=(B, T // tb),
        in_specs=[kspec, kspec, kspec, vspec, sspec],
        out_specs=[vspec, sspec],
        out_shape=[jax.ShapeDtypeStruct((B, T, d), F32),
                   jax.ShapeDtypeStruct((B, heads, dk, dv), F32)],
        scratch_shapes=[pltpu.VMEM((heads, dv, dk), F32)],
        compiler_params=_params(("parallel", "arbitrary")),
        name="gla_chunk",
    )(q, k, g, v, s0)


def _gla_post_kernel(o_ref, gate_ref, x_ref, mod_ref, gnw_ref, wo_ref, fg_ref, out_ref, *, final, heads):
    bb, tt, d = x_ref.shape
    m = bb * tt
    hs = d // heads
    o = o_ref[...].reshape(m, d)
    ms = _head_bcast(_head_sum(o * o, hs) * (1.0 / hs), d, hs)
    y = o * lax.rsqrt(ms + NORM_EPS) * gnw_ref[...]
    y = y * _silu(gate_ref[...].reshape(m, d))
    _residual_out(y, wo_ref, x_ref, mod_ref, fg_ref, out_ref, final)


def _gla_layer(x, mod, norm_g, final_g, final, s0, p):
    d = x.shape[-1]
    qk = p['w_gk2'].shape[1]
    w_in = p['w_in']
    consts = [norm_g.reshape(1, d), w_in[:, :2 * qk + 2 * d].astype(BF16),
              _pad_cols(w_in[:, 2 * qk + 2 * d:], LANES).astype(BF16), _pad_rows(p['w_gk2'], LANES),
              p['b_gk2'].reshape(1, qk)]
    q, k, v, gate, gk = _pre_call(functools.partial(_gla_pre_kernel, qk=qk), "gla_pre", x, mod, consts,
                                  (qk, qk, d, d, qk))
    o, s_new = _gla_chunk(q, k, gk, v, s0, GLA_HEADS)
    post_consts = [jnp.tile(p['gn_w'], GLA_HEADS).reshape(1, d), p['w_o'].astype(BF16), final_g.reshape(1, d)]
    x = _post_call(functools.partial(_gla_post_kernel, heads=GLA_HEADS), "gla_post", x, mod, [o, gate], post_consts,
                   final)
    return x, s_new


def _hg_layer(x, mod, norm_g, final_g, final, s0, layer, p):
    d = x.shape[-1]
    consts = [norm_g.reshape(1, d), p['w_in'].astype(BF16), p['lower']]
    q, k, v, gate, g = _pre_call(functools.partial(_hg_pre_kernel, layer=layer), "hgrn_pre", x, mod, consts,
                                 (d, d, d, d, d))
    o, s_new = _gla_chunk(q, k, g, v, s0, HG_HEADS)
    post_consts = [jnp.tile(p['gn_w'], HG_HEADS).reshape(1, d), p['w_o'].astype(BF16), final_g.reshape(1, d)]
    x = _post_call(functools.partial(_gla_post_kernel, heads=HG_HEADS), "hgrn_post", x, mod, [o, gate], post_consts,
                   final)
    return x, s_new


def _sb_pre_kernel(x_ref, mod_ref, ng_ref, w_ref, q_out, k_out, v_out, gate_out):
    bb, tt, d = x_ref.shape
    m = bb * tt
    h = _prenorm(x_ref[...], ng_ref[...], mod_ref[...], d).reshape(m, d).astype(BF16)
    hd = d // SB_HEADS
    q_out[...] = (_dot(h, w_ref[:, 0:d]) * hd ** -0.5).reshape(bb, tt, d)
    k_out[...] = _dot(h, w_ref[:, d:2 * d]).reshape(bb, tt, d)
    v_out[...] = _dot(h, w_ref[:, 2 * d:3 * d]).reshape(bb, tt, d)
    gate_out[...] = _dot(h, w_ref[:, 3 * d:4 * d]).reshape(bb, tt, d)


def _sb_prompt_kernel(bias_ref, q_ref, k_ref, v_ref, o_ref, acc_scr, z_scr, sp_scr, *, tq):
    p = pl.program_id(1)
    i = pl.program_id(2)
    half = LANES // 2
    q = q_ref[0] * LOG2E
    lane_a = _iota((tq, LANES), 1) < half
    q_st = jnp.concatenate([jnp.where(lane_a, q, 0.0), jnp.where(lane_a, 0.0, q)], axis=0).astype(BF16)
    bias = jnp.where(_iota((2 * tq, 1), 0) < tq, bias_ref[2 * p], bias_ref[2 * p + 1]) * LOG2E
    tri = jnp.where(_iota((tq, tq), 0) >= _iota((tq, tq), 1), 1.0, 0.0).astype(BF16)

    def block_rows(jb):
        return pl.ds(pl.multiple_of(jnp.maximum(jb, 0) * tq, tq), tq)

    def scores(jb):
        z2 = _dot(q_st, k_ref[0, block_rows(jb), :].astype(BF16), _NT) + bias
        sp2 = jnp.maximum(z2, 0.0) + jnp.log2(1.0 + jnp.exp2(-jnp.abs(z2)))
        return z2, sp2

    def score_into(slot, jb):
        z2, sp2 = scores(jb)
        z_scr[slot] = z2
        sp_scr[slot] = sp2.astype(BF16)

    def finish(slot, jb, carry):
        cs = _dot(sp_scr[slot], tri)
        w = jnp.exp2(z_scr[slot] - cs - carry)
        acc_scr[...] += _dot(w.astype(BF16), v_ref[0, block_rows(jb), :].astype(BF16))
        return carry + cs[:, 0:1]

    score_into(0, i - 1)

    z2, sp2 = scores(i)
    before = _iota((2 * tq, tq), 1) < _iota((2 * tq, tq), 0) % tq
    spm = jnp.where(before, sp2, 0.0).astype(BF16)
    cs = _dot(spm, tri)
    w = jnp.where(before, jnp.exp2((z2 - sp2) - (cs - spm.astype(F32))), 0.0)
    acc_scr[...] = _dot(w.astype(BF16), v_ref[0, block_rows(i), :].astype(BF16))
    carry = cs[:, 0:1]

    def pair(n, carry):
        jb = i - 1 - 2 * n
        score_into(1, jb - 1)
        carry = finish(0, jb, carry)
        score_into(0, jb - 2)
        return finish(1, jb - 1, carry)

    carry = lax.fori_loop(0, i // 2, pair, carry)

    @pl.when(i % 2 == 1)
    def _():
        finish(0, 0, carry)

    acc = acc_scr[...]
    o_ref[0] = jnp.where(lane_a, acc[0:tq], acc[tq:2 * tq])


def _sb_prompt(q, k, v, bias):
    B, T, d = q.shape
    tq = min(T, 256)
    qspec = pl.BlockSpec((1, tq, LANES), lambda b, p, i: (b, i, p))
    kspec = pl.BlockSpec((1, T, LANES), lambda b, p, i: (b, 0, p))
    return pl.pallas_call(
        functools.partial(_sb_prompt_kernel, tq=tq),
        grid=(B, d // LANES, T // tq),
        in_specs=[pl.BlockSpec(memory_space=pltpu.SMEM), qspec, kspec, kspec],
        out_specs=qspec,
        out_shape=jax.ShapeDtypeStruct((B, T, d), F32),
        scratch_shapes=[pltpu.VMEM((2 * tq, LANES), F32), pltpu.VMEM((2, 2 * tq, tq), F32),
                        pltpu.VMEM((2, 2 * tq, tq), BF16)],
        compiler_params=_params(("parallel", "parallel", "arbitrary")),
        name="sb_attention_prompt",
    )(bias, q, k, v)


def _sb_paged_kernel(pt_ref, q_ref, kn_ref, vn_ref, bias_ref, tri_ref, *refs, n_blocks):
    kc_refs, vc_refs = refs[:n_blocks], refs[n_blocks:2 * n_blocks]
    o_ref, acc_scr, carry_scr, qbd_scr = refs[2 * n_blocks:]
    t, d = q_ref.shape[1], q_ref.shape[2]
    heads = LANES // t
    hd = d // heads
    j = pl.program_id(1)
    bias = bias_ref[...] * LOG2E

    def attend(k, v, nk, before, carry, qk_dims, pv_dims):
        z2 = _dot(qbd_scr[...], k, qk_dims) + bias[:, 0:nk]
        sp2 = jnp.maximum(z2, 0.0) + jnp.log2(1.0 + jnp.exp2(-jnp.abs(z2)))
        spb = (sp2 if before is None else jnp.where(before, sp2, 0.0)).astype(BF16)
        cs = _dot(spb, tri_ref[0:nk, 0:nk])
        if before is None:
            w = jnp.exp2(z2 - cs - jnp.concatenate([carry] * (nk // LANES), axis=1))
        else:
            w = jnp.where(before, jnp.exp2((z2 - sp2) - (cs - spb.astype(F32))), 0.0)
        total = _dot(spb, jnp.ones((nk, LANES), BF16))
        return _dot(w.astype(BF16), v, pv_dims), total

    @pl.when(j == 0)
    def _():
        head_of_row = _iota((LANES, d), 0) // t
        head_of_lane = _iota((LANES, d), 1) // hd
        q_rows = jnp.concatenate([q_ref[0] * LOG2E] * heads, axis=0)
        qbd_scr[...] = jnp.where(head_of_lane == head_of_row, q_rows, 0.0).astype(BF16)
        nk = 2 * t
        pad = jnp.zeros((nk - t, d), F32)
        before = _iota((LANES, nk), 1) < _iota((LANES, nk), 0) % t
        upd, total = attend(jnp.concatenate([kn_ref[0], pad], axis=0).astype(BF16),
                            jnp.concatenate([vn_ref[0], pad], axis=0).astype(BF16), nk, before, None, _NT, _NN)
        acc_scr[...] = upd
        carry_scr[...] = total

    page = kc_refs[0].shape[2]

    def joined(page_refs):
        return jnp.concatenate([r[...].reshape(d, page) for r in reversed(page_refs)], axis=1).astype(BF16)

    upd, total = attend(joined(kc_refs), joined(vc_refs), n_blocks * page, None, carry_scr[...], _NN, _NT)
    acc_scr[...] += upd
    carry_scr[...] += total

    @pl.when(j == pl.num_programs(1) - 1)
    def _():
        full = acc_scr[...]
        head_of_lane = _iota((t, d), 1) // hd
        out = jnp.zeros((t, d), F32)
        for h in range(heads):
            out = jnp.where(head_of_lane == h, full[h * t:(h + 1) * t, :], out)
        o_ref[0] = out


def _sb_paged(q, k_new, v_new, bias, k_cache, v_cache, layer, page_table):
    B, T, d = q.shape
    _, _, page, heads, hd = k_cache.shape
    n_pages = page_table.shape[1]
    n_blocks = max(n for n in (8, 4, 2, 1) if n_pages % n == 0)
    assert heads * T == LANES and heads * hd == d
    tok = pl.BlockSpec((1, T, d), lambda b, j, pt: (b, 0, 0))

    def cache(i):
        return pl.BlockSpec((None, None, heads, hd, page),
                            lambda b, j, pt: (layer, pt[b, n_pages - 1 - (j * n_blocks + i)], 0, 0, 0))

    nk = n_blocks * page
    bias_rows = jnp.broadcast_to(jnp.repeat(bias, T)[:, None], (LANES, nk))
    tri = (jnp.arange(nk)[:, None] >= jnp.arange(nk)[None, :]).astype(BF16)
    grid_spec = pltpu.PrefetchScalarGridSpec(
        num_scalar_prefetch=1,
        grid=(B, n_pages // n_blocks),
        in_specs=[tok, tok, tok, pl.BlockSpec((LANES, nk), lambda b, j, pt: (0, 0)),
                  pl.BlockSpec((nk, nk), lambda b, j, pt: (0, 0))]
        + [cache(i) for i in range(n_blocks)] * 2,
        out_specs=tok,
        scratch_shapes=[pltpu.VMEM((LANES, d), F32), pltpu.VMEM((LANES, LANES), F32), pltpu.VMEM((LANES, d), BF16)],
    )
    return pl.pallas_call(
        functools.partial(_sb_paged_kernel, n_blocks=n_blocks),
        grid_spec=grid_spec,
        out_shape=jax.ShapeDtypeStruct((B, T, d), F32),
        compiler_params=_params(("parallel", "arbitrary")),
        name="sb_attention_paged",
    )(page_table, q, k_new, v_new, bias_rows, tri, *([jnp.transpose(k_cache, (0, 1, 3, 4, 2))] * n_blocks),
      *([jnp.transpose(v_cache, (0, 1, 3, 4, 2))] * n_blocks))


def _sb_post_kernel(o_ref, gate_ref, x_ref, mod_ref, wo_ref, fg_ref, out_ref, *, final):
    bb, tt, d = x_ref.shape
    m = bb * tt
    y = o_ref[...].reshape(m, d) * _silu(gate_ref[...].reshape(m, d))
    _residual_out(y, wo_ref, x_ref, mod_ref, fg_ref, out_ref, final)


def _sb_layer(x, mod, norm_g, final_g, final, past, p):
    B, T, d = x.shape
    consts = [norm_g.reshape(1, d), p['w_in'].astype(BF16)]
    q, k, v, gate = _pre_call(_sb_pre_kernel, "sb_pre", x, mod, consts, (d, d, d, d))
    if past is None:
        o = _sb_prompt(q, k, v, p['bias'])
    else:
        o = _sb_paged(q, k, v, p['bias'], *past)
    x = _post_call(_sb_post_kernel, "sb_post", x, mod, [o, gate], [p['w_o'].astype(BF16), final_g.reshape(1, d)],
                   final)
    hd = d // SB_HEADS
    return x, k.reshape(B, T, SB_HEADS, hd), v.reshape(B, T, SB_HEADS, hd)


def kernel(x_prompt, x_sample, c_prompt, c_sample, state_rwkv, cache_rwkv_shift, state_gla, cache_sb_k, cache_sb_v, state_hgrn, page_table, norm_g, ada_w, ada_b, final_g, rw_mix, rw_w_rkvg, rw_w0, rw_w1, rw_w2, rw_a0, rw_a1, rw_a2, rw_k_k, rw_k_a, rw_r_k, rw_gn_w, rw_gn_b, rw_w_o, gla_w_in, gla_w_gk2, gla_b_gk2, gla_gn_w, gla_w_o, sb_w_in, sb_bias, sb_w_o, hg_w_in, hg_lower, hg_gn_w, hg_w_o):
    depth, d = norm_g.shape
    n_mix = 4
    bp, bs = x_prompt.shape[0], x_sample.shape[0]
    rows = -(-(bp + bs) // 8) * 8
    c_all = jnp.pad(jnp.concatenate([c_prompt, c_sample], axis=0), ((0, rows - bp - bs), (0, 0)))
    mods = _modulation(c_all, ada_w, ada_b)

    def trunk(x, mod_all, rw_s, rw_shift, gla_s, sb_past, hg_s):
        B = x.shape[0]
        outs = {n: [] for n in ('rw_s', 'rw_shift', 'gla_s', 'sb_k', 'sb_v', 'hg_s')}
        for i in range(depth):
            kind, j = i % n_mix, i // n_mix
            mod = mod_all[i].reshape(B, 1, 3 * d)
            final = i == depth - 1
            if kind == 0:
                p = dict(mix=rw_mix[j], w_rkvg=rw_w_rkvg[j], w0=rw_w0[j], w1=rw_w1[j], w2=rw_w2[j], a0=rw_a0[j],
                         a1=rw_a1[j], a2=rw_a2[j], k_k=rw_k_k[j], k_a=rw_k_a[j], r_k=rw_r_k[j], gn_w=rw_gn_w[j],
                         gn_b=rw_gn_b[j], w_o=rw_w_o[j])
                x, s, sh = _rwkv_layer(x, mod, norm_g[i], final_g, final, rw_shift[j], rw_s[j], p)
                outs['rw_s'].append(s)
                outs['rw_shift'].append(sh)
            elif kind == 1:
                p = dict(w_in=gla_w_in[j], w_gk2=gla_w_gk2[j], b_gk2=gla_b_gk2[j], gn_w=gla_gn_w[j], w_o=gla_w_o[j])
                x, s = _gla_layer(x, mod, norm_g[i], final_g, final, gla_s[j], p)
                outs['gla_s'].append(s)
            elif kind == 2:
                p = dict(w_in=sb_w_in[j], bias=sb_bias[j], w_o=sb_w_o[j])
                x, k_new, v_new = _sb_layer(x, mod, norm_g[i], final_g, final, sb_past(j), p)
                outs['sb_k'].append(k_new)
                outs['sb_v'].append(v_new)
            else:
                p = dict(w_in=hg_w_in[j], lower=hg_lower, gn_w=hg_gn_w[j], w_o=hg_w_o[j])
                x, s = _hg_layer(x, mod, norm_g[i], final_g, final, hg_s[j], i, p)
                outs['hg_s'].append(s)
        st = lambda n: jnp.stack(outs[n])
        return x, st('rw_s'), st('rw_shift'), st('gla_s'), st('sb_k'), st('sb_v'), st('hg_s')

    z = lambda a, b: jnp.zeros((a.shape[0], b) + a.shape[2:], a.dtype)
    prompt = trunk(x_prompt, mods[:, :bp], z(state_rwkv, bp), z(cache_rwkv_shift, bp), z(state_gla, bp),
                   lambda j: None, z(state_hgrn, bp))
    sample = trunk(x_sample, mods[:, bp:bp + bs], state_rwkv, cache_rwkv_shift, state_gla,
                   lambda j: (cache_sb_k, cache_sb_v, j, page_table), state_hgrn)
    return (prompt[0], sample[0]) + prompt[1:] + sample[1:]
```
